```python
import jax, jax.numpy as jnp
from jax import lax
import numpy as np

D_MODEL = 1024
BATCH = 2
SEQ = 8192
DEPTH = 1
DEC_BATCH = 128
DEC_SEQ = 4
PAST_LEN = 16384
PAGE_SIZE = 128

FOX_HEADS = 8
FOX_KV_HEADS = 4
FOX_HEAD_DIM = 64
MLA_HEADS = 4
MLA_Q_LORA = 256
MLA_KV_LORA = 256
MLA_NOPE_DIM = 128
MLA_ROPE_DIM = 64
MLA_V_DIM = 128
ROPE_THETA = 10000.0
MEM_TOKENS = 256
MEM_HEADS = 4
MEM_HEAD_DIM = 128
D_FF = 2816
Q_BLOCK = 128
RMS_EPS = 1e-6
NEG_INF = -1e30

FOX_Q_W = FOX_HEADS * FOX_HEAD_DIM
FOX_KV_W = FOX_KV_HEADS * FOX_HEAD_DIM
IN_W = FOX_Q_W + 2 * FOX_KV_W + FOX_HEADS + MLA_Q_LORA + MLA_KV_LORA + MLA_ROPE_DIM
MIX_W = FOX_Q_W + MLA_HEADS * MLA_V_DIM
MEM_W = MEM_HEADS * MEM_HEAD_DIM

kernel_name = 'hymba_fox_mla_macaron_decode_step'


def rmsnorm(x, g):
    xf = x.astype(jnp.float32)
    y = xf * lax.rsqrt(jnp.mean(xf * xf, axis=-1, keepdims=True) + RMS_EPS)
    return (y * g.astype(jnp.float32)).astype(x.dtype)


def swiglu(h, w_gu, w_down):
    gate, up = jnp.split(h @ w_gu, 2, axis=-1)
    return (jax.nn.silu(gate) * up) @ w_down


def ffn_half(x, g_pre, w_gu, w_down, g_post):
    return x + 0.5 * rmsnorm(swiglu(rmsnorm(x, g_pre), w_gu, w_down), g_post)


def rope(x, pos):
    half = x.shape[-1] // 2
    inv_freq = ROPE_THETA ** (-jnp.arange(half, dtype=jnp.float32) / half)
    ang = pos.astype(jnp.float32)[:, None] * inv_freq[None, :]
    cos, sin = jnp.cos(ang)[:, None, :], jnp.sin(ang)[:, None, :]
    x1 = x[..., :half].astype(jnp.float32)
    x2 = x[..., half:].astype(jnp.float32)
    return jnp.concatenate([x1 * cos - x2 * sin, x2 * cos + x1 * sin], axis=-1).astype(x.dtype)


def mix_projections(h, pos, w_in, b_fgate, g_q_norm, w_q_up, g_kv_norm, w_kv_up):
    B, S, _ = h.shape
    cuts = np.cumsum([FOX_Q_W, FOX_KV_W, FOX_KV_W, FOX_HEADS, MLA_Q_LORA, MLA_KV_LORA]).tolist()
    q_f, k_f, v_f, f_logit, cq, ckv, kr = jnp.split(h @ w_in, cuts, axis=-1)
    q_f = q_f.reshape(B, S, FOX_HEADS, FOX_HEAD_DIM)
    k_f = k_f.reshape(B, S, FOX_KV_HEADS, FOX_HEAD_DIM)
    v_f = v_f.reshape(B, S, FOX_KV_HEADS, FOX_HEAD_DIM)
    logf = jax.nn.log_sigmoid((f_logit + b_fgate).astype(jnp.float32))
    q_m = (rmsnorm(cq, g_q_norm) @ w_q_up).reshape(B, S, MLA_HEADS, MLA_NOPE_DIM + MLA_ROPE_DIM)
    w_uk = w_kv_up.reshape(MLA_KV_LORA, MLA_HEADS, MLA_NOPE_DIM + MLA_V_DIM)[..., :MLA_NOPE_DIM]
    q_lat = jnp.einsum('bshn,chn->bshc', q_m[..., :MLA_NOPE_DIM], w_uk)
    q_rope = rope(q_m[..., MLA_NOPE_DIM:], pos)
    ckv = rmsnorm(ckv, g_kv_norm)
    kr = rope(kr[:, :, None, :], pos)[:, :, 0, :]
    return q_f, k_f, v_f, logf, q_lat, q_rope, ckv, kr


def fox_attend(q, k, v, c_q, c_k, q_pos, k_pos):
    B, Q, H, d = q.shape
    K = k.shape[1]
    G = H // FOX_KV_HEADS
    qg = q.reshape(B, Q, FOX_KV_HEADS, G, d)
    s = jnp.einsum('bqkgd,bskd->bkgqs', qg, k, preferred_element_type=jnp.float32) * (d ** -0.5)
    bias = (c_q.reshape(B, Q, FOX_KV_HEADS, G).transpose(0, 2, 3, 1)[..., None]
            - c_k.reshape(B, K, FOX_KV_HEADS, G).transpose(0, 2, 3, 1)[..., None, :])
    s = jnp.where(q_pos[:, None] >= k_pos[None, :], s + bias, NEG_INF)
    p = jax.nn.softmax(s, axis=-1)
    o = jnp.einsum('bkgqs,bskd->bqkgd', p.astype(v.dtype), v)
    return o.reshape(B, Q, H, d)


def mla_attend(q_lat, q_rope, ckv, kr, q_pos, k_pos):
    scale = (MLA_NOPE_DIM + MLA_ROPE_DIM) ** -0.5
    s = (jnp.einsum('bqhc,bkc->bhqk', q_lat, ckv, preferred_element_type=jnp.float32)
         + jnp.einsum('bqhr,bkr->bhqk', q_rope, kr, preferred_element_type=jnp.float32)) * scale
    s = jnp.where(q_pos[:, None] >= k_pos[None, :], s, NEG_INF)
    p = jax.nn.softmax(s, axis=-1)
    return jnp.einsum('bhqk,bkc->bqhc', p.astype(ckv.dtype), ckv)


def mix_output(o_fox, o_lat, w_kv_up, w_out):
    B, S = o_fox.shape[:2]
    w_uv = w_kv_up.reshape(MLA_KV_LORA, MLA_HEADS, MLA_NOPE_DIM + MLA_V_DIM)[..., MLA_NOPE_DIM:]
    o_mla = jnp.einsum('bshc,chv->bshv', o_lat, w_uv)
    o = jnp.concatenate([o_fox.reshape(B, S, FOX_Q_W), o_mla.reshape(B, S, MLA_HEADS * MLA_V_DIM)], axis=-1)
    return o @ w_out


def take_block(a, start):
    return lax.dynamic_slice_in_dim(a, start, Q_BLOCK, axis=1)


def take_pos(pos, start):
    return lax.dynamic_slice_in_dim(pos, start, Q_BLOCK, axis=0)


def sweep_query_blocks(attend, n_q):
    out = lax.map(attend, jnp.arange(n_q // Q_BLOCK, dtype=jnp.int32) * Q_BLOCK)
    return jnp.moveaxis(out, 0, 1).reshape((out.shape[1], n_q) + out.shape[3:])


def gather_pages(pool, layer, page_table):
    g = pool[layer, page_table]
    return g.reshape((g.shape[0], g.shape[1] * g.shape[2]) + g.shape[3:])


def memory_kv(mem, g_mem_tok, w_mem_kv):
    B, M, _ = mem.shape
    k, v = jnp.split(rmsnorm(mem, g_mem_tok) @ w_mem_kv, 2, axis=-1)
    return k.reshape(B, M, MEM_HEADS, MEM_HEAD_DIM), v.reshape(B, M, MEM_HEADS, MEM_HEAD_DIM)


def memory_attend(x, mk, mv, g_pre, w_mem_q, w_mem_o, g_post):
    B, S, _ = x.shape
    q = (rmsnorm(x, g_pre) @ w_mem_q).reshape(B, S, MEM_HEADS, MEM_HEAD_DIM)
    s = jnp.einsum('bqhd,bmhd->bhqm', q, mk, preferred_element_type=jnp.float32) * (MEM_HEAD_DIM ** -0.5)
    p = jax.nn.softmax(s, axis=-1)
    o = jnp.einsum('bhqm,bmhd->bqhd', p.astype(mv.dtype), mv).reshape(B, S, MEM_W)
    return x + rmsnorm(o @ w_mem_o, g_post)


def setup_inputs(seed: int = 0) -> dict:
    key = jax.random.key(seed)
    ks = jax.random.split(key, 34)

    def nrm(i, shape, scale=1.0):
        return jax.random.normal(ks[i], shape, jnp.float32) * scale

    def gain(i, n):
        return 1.0 + 0.02 * jax.random.normal(ks[i], (DEPTH, n), jnp.float32)

    n_pages = PAST_LEN // PAGE_SIZE
    n_used = DEC_BATCH * n_pages
    n_pool = n_used + n_used // 4
    page_table = jax.random.permutation(ks[10], n_pool)[:n_used].reshape(DEC_BATCH, n_pages).astype(jnp.int32)
    D = D_MODEL
    return {
        'x_prompt': nrm(0, (BATCH, SEQ, D)),
        'x_sample': nrm(1, (DEC_BATCH, DEC_SEQ, D)),
        'mem_prompt': nrm(2, (BATCH, MEM_TOKENS, D)),
        'cache_fox_k': nrm(3, (DEPTH, n_pool, PAGE_SIZE, FOX_KV_HEADS, FOX_HEAD_DIM)),
        'cache_fox_v': nrm(4, (DEPTH, n_pool, PAGE_SIZE, FOX_KV_HEADS, FOX_HEAD_DIM)),
        'cache_fox_logf': jax.nn.log_sigmoid(3.0 + nrm(5, (DEPTH, n_pool, PAGE_SIZE, FOX_HEADS))),
        'cache_mla_ckv': nrm(6, (DEPTH, n_pool, PAGE_SIZE, MLA_KV_LORA)),
        'cache_mla_krope': nrm(7, (DEPTH, n_pool, PAGE_SIZE, MLA_ROPE_DIM)),
        'cache_mem_k': nrm(8, (DEPTH, DEC_BATCH, MEM_TOKENS, MEM_HEADS, MEM_HEAD_DIM)),
        'cache_mem_v': nrm(9, (DEPTH, DEC_BATCH, MEM_TOKENS, MEM_HEADS, MEM_HEAD_DIM)),
        'page_table': page_table,
        'g_ffn1_pre': gain(11, D),
        'w_ffn1_gu': nrm(12, (DEPTH, D, 2 * D_FF), D ** -0.5),
        'w_ffn1_down': nrm(13, (DEPTH, D_FF, D), D_FF ** -0.5),
        'g_ffn1_post': gain(14, D),
        'g_mix_pre': gain(15, D),
        'w_in': nrm(16, (DEPTH, D, IN_W), D ** -0.5),
        'b_fgate': 3.0 + nrm(17, (DEPTH, FOX_HEADS), 0.1),
        'g_q_norm': gain(18, MLA_Q_LORA),
        'w_q_up': nrm(19, (DEPTH, MLA_Q_LORA, MLA_HEADS * (MLA_NOPE_DIM + MLA_ROPE_DIM)), MLA_Q_LORA ** -0.5),
        'g_kv_norm': gain(20, MLA_KV_LORA),
        'w_kv_up': nrm(21, (DEPTH, MLA_KV_LORA, MLA_HEADS * (MLA_NOPE_DIM + MLA_V_DIM)), MLA_KV_LORA ** -0.5),
        'w_out': nrm(22, (DEPTH, MIX_W, D), MIX_W ** -0.5),
        'g_mix_post': gain(23, D),
        'g_mem_tok': gain(24, D),
        'w_mem_kv': nrm(25, (DEPTH, D, 2 * MEM_W), D ** -0.5),
        'g_mem_pre': gain(26, D),
        'w_mem_q': nrm(27, (DEPTH, D, MEM_W), D ** -0.5),
        'w_mem_o': nrm(28, (DEPTH, MEM_W, D), MEM_W ** -0.5),
        'g_mem_post': gain(29, D),
        'g_ffn2_pre': gain(30, D),
        'w_ffn2_gu': nrm(31, (DEPTH, D, 2 * D_FF), D ** -0.5),
        'w_ffn2_down': nrm(32, (DEPTH, D_FF, D), D_FF ** -0.5),
        'g_ffn2_post': gain(33, D),
    }


def reference(x_prompt, x_sample, mem_prompt, cache_fox_k, cache_fox_v, cache_fox_logf,
              cache_mla_ckv, cache_mla_krope, cache_mem_k, cache_mem_v, page_table,
              g_ffn1_pre, w_ffn1_gu, w_ffn1_down, g_ffn1_post,
              g_mix_pre, w_in, b_fgate, g_q_norm, w_q_up, g_kv_norm, w_kv_up, w_out, g_mix_post,
              g_mem_tok, w_mem_kv, g_mem_pre, w_mem_q, w_mem_o, g_mem_post,
              g_ffn2_pre, w_ffn2_gu, w_ffn2_down, g_ffn2_post):
    pos_p = jnp.arange(SEQ, dtype=jnp.int32)
    pos_s = PAST_LEN + jnp.arange(DEC_SEQ, dtype=jnp.int32)
    pos_all = jnp.arange(PAST_LEN + DEC_SEQ, dtype=jnp.int32)
    xp, xs = x_prompt, x_sample
    pk, pv, plf, pckv, pkr, pmk, pmv = [], [], [], [], [], [], []
    sk, sv, slf, sckv, skr = [], [], [], [], []
    for l in range(DEPTH):
        xp = ffn_half(xp, g_ffn1_pre[l], w_ffn1_gu[l], w_ffn1_down[l], g_ffn1_post[l])
        xs = ffn_half(xs, g_ffn1_pre[l], w_ffn1_gu[l], w_ffn1_down[l], g_ffn1_post[l])

        qf, kf, vf, lf, ql, qr, ckv, kr = mix_projections(
            rmsnorm(xp, g_mix_pre[l]), pos_p, w_in[l], b_fgate[l], g_q_norm[l], w_q_up[l], g_kv_norm[l], w_kv_up[l])
        cf = jnp.cumsum(lf, axis=1)
        o_fox = sweep_query_blocks(
            lambda s0: fox_attend(take_block(qf, s0), kf, vf, take_block(cf, s0), cf, take_pos(pos_p, s0), pos_p), SEQ)
        o_lat = sweep_query_blocks(
            lambda s0: mla_attend(take_block(ql, s0), take_block(qr, s0), ckv, kr, take_pos(pos_p, s0), pos_p), SEQ)
        xp = xp + rmsnorm(mix_output(o_fox, o_lat, w_kv_up[l], w_out[l]), g_mix_post[l])
        pk.append(kf)
        pv.append(vf)
        plf.append(lf)
        pckv.append(ckv)
        pkr.append(kr)

        qf, kf, vf, lf, ql, qr, ckv, kr = mix_projections(
            rmsnorm(xs, g_mix_pre[l]), pos_s, w_in[l], b_fgate[l], g_q_norm[l], w_q_up[l], g_kv_norm[l], w_kv_up[l])
        k_all = jnp.concatenate([gather_pages(cache_fox_k, l, page_table), kf], axis=1)
        v_all = jnp.concatenate([gather_pages(cache_fox_v, l, page_table), vf], axis=1)
        c_all = jnp.cumsum(jnp.concatenate(
            [gather_pages(cache_fox_logf, l, page_table).astype(jnp.float32), lf], axis=1), axis=1)
        o_fox = fox_attend(qf, k_all, v_all, c_all[:, PAST_LEN:], c_all, pos_s, pos_all)
        ckv_all = jnp.concatenate([gather_pages(cache_mla_ckv, l, page_table), ckv], axis=1)
        kr_all = jnp.concatenate([gather_pages(cache_mla_krope, l, page_table), kr], axis=1)
        o_lat = mla_attend(ql, qr, ckv_all, kr_all, pos_s, pos_all)
        xs = xs + rmsnorm(mix_output(o_fox, o_lat, w_kv_up[l], w_out[l]), g_mix_post[l])
        sk.append(kf)
        sv.append(vf)
        slf.append(lf)
        sckv.append(ckv)
        skr.append(kr)

        mk, mv = memory_kv(mem_prompt, g_mem_tok[l], w_mem_kv[l])
        xp = memory_attend(xp, mk, mv, g_mem_pre[l], w_mem_q[l], w_mem_o[l], g_mem_post[l])
        xs = memory_attend(xs, cache_mem_k[l], cache_mem_v[l], g_mem_pre[l], w_mem_q[l], w_mem_o[l], g_mem_post[l])
        pmk.append(mk)
        pmv.append(mv)

        xp = ffn_half(xp, g_ffn2_pre[l], w_ffn2_gu[l], w_ffn2_down[l], g_ffn2_post[l])
        xs = ffn_half(xs, g_ffn2_pre[l], w_ffn2_gu[l], w_ffn2_down[l], g_ffn2_post[l])

    y_prompt, y_sample = xp, xs
    p_fox_k, p_fox_v, p_fox_logf = jnp.stack(pk), jnp.stack(pv), jnp.stack(plf)
    p_mla_ckv, p_mla_krope = jnp.stack(pckv), jnp.stack(pkr)
    p_mem_k, p_mem_v = jnp.stack(pmk), jnp.stack(pmv)
    s_fox_k, s_fox_v, s_fox_logf = jnp.stack(sk), jnp.stack(sv), jnp.stack(slf)
    s_mla_ckv, s_mla_krope = jnp.stack(sckv), jnp.stack(skr)
    return (y_prompt, y_sample, p_fox_k, p_fox_v, p_fox_logf, p_mla_ckv, p_mla_krope, p_mem_k, p_mem_v,
            s_fox_k, s_fox_v, s_fox_logf, s_mla_ckv, s_mla_krope)
```

```python
import functools

import numpy as np
import jax
import jax.numpy as jnp
from jax import lax
from jax.experimental import pallas as pl
from jax.experimental.pallas import tpu as pltpu

BF = jnp.bfloat16
F32 = jnp.float32

D_MODEL = 1024
PAGE_SIZE = 128
FOX_HEADS = 8
FOX_KV_HEADS = 4
FOX_GROUP = FOX_HEADS // FOX_KV_HEADS
FOX_HEAD_DIM = 64
FOX_Q_W = FOX_HEADS * FOX_HEAD_DIM
FOX_KV_W = FOX_KV_HEADS * FOX_HEAD_DIM
MLA_HEADS = 4
MLA_Q_LORA = 256
MLA_KV_LORA = 256
MLA_NOPE_DIM = 128
MLA_ROPE_DIM = 64
MLA_V_DIM = 128
ROPE_THETA = 10000.0
MEM_HEADS = 4
MEM_HEAD_DIM = 128
MEM_W = MEM_HEADS * MEM_HEAD_DIM
D_FF = 2816
RMS_EPS = 1e-6
NEG_INF = -1e30

LANES = 128
VMEM_LIMIT_BYTES = 56 * 1024 * 1024

FOX_SCALE = FOX_HEAD_DIM ** -0.5
MLA_SCALE = (MLA_NOPE_DIM + MLA_ROPE_DIM) ** -0.5
MEM_SCALE = MEM_HEAD_DIM ** -0.5

ROW_TILE = 512
FF_CHUNK = 256
ATT_TQ = 256
ATT_TK = 256
DEC_PAGES = 16
MEM_DEC_BATCH = 8


def _params(*sem):
    return pltpu.CompilerParams(dimension_semantics=sem,
                                vmem_limit_bytes=VMEM_LIMIT_BYTES)


def _rms(x, g):
    return x * lax.rsqrt(jnp.mean(x * x, axis=-1, keepdims=True) + RMS_EPS) * g


def _dot(a, b):
    return jnp.dot(a, b, preferred_element_type=F32)


def _dot_nt(a, b):
    return lax.dot_general(a, b, (((1,), (1,)), ((), ())), preferred_element_type=F32)


def _const_spec(shape):
    zeros = (0,) * len(shape)
    return pl.BlockSpec(shape, lambda *_: zeros)


def _row_tile(n):
    return ROW_TILE if n % ROW_TILE == 0 else n


def _swiglu_half(x, gpre, wg_ref, wu_ref, wd_ref, gpost):
    h = _rms(x, gpre).astype(BF)
    acc = jnp.zeros(x.shape, F32)
    for c in range(D_FF // FF_CHUNK):
        sl = slice(c * FF_CHUNK, (c + 1) * FF_CHUNK)
        g = _dot(h, wg_ref[:, sl])
        u = _dot(h, wu_ref[:, sl])
        a = (g * jax.nn.sigmoid(g) * u).astype(BF)
        acc = acc + _dot(a, wd_ref[sl, :])
    return x + 0.5 * _rms(acc, gpost)


def _ffn_kernel(x_ref, gpre_ref, wg_ref, wu_ref, wd_ref, gpost_ref, o_ref):
    o_ref[...] = _swiglu_half(x_ref[...], gpre_ref[...], wg_ref, wu_ref, wd_ref,
                              gpost_ref[...])


def _ffn_half(x, gpre, wg, wu, wd, gpost):
    n = x.shape[0]
    tm = _row_tile(n)
    row = pl.BlockSpec((tm, D_MODEL), lambda i: (i, 0))
    return pl.pallas_call(
        _ffn_kernel,
        grid=(n // tm,),
        in_specs=[row, _const_spec((1, D_MODEL)), _const_spec((D_MODEL, D_FF)),
                  _const_spec((D_MODEL, D_FF)), _const_spec((D_FF, D_MODEL)),
                  _const_spec((1, D_MODEL))],
        out_specs=row,
        out_shape=jax.ShapeDtypeStruct((n, D_MODEL), F32),
        compiler_params=_params("parallel"),
        name="ffn_half",
    )(x, gpre, wg, wu, wd, gpost)


def _memout_ffn_kernel(x_ref, om_ref, wmo_ref, gmpost_ref, gpre_ref, wg_ref, wu_ref,
                       wd_ref, gpost_ref, o_ref):
    x = x_ref[...] + _rms(_dot(om_ref[...], wmo_ref[...]), gmpost_ref[...])
    o_ref[...] = _swiglu_half(x, gpre_ref[...], wg_ref, wu_ref, wd_ref, gpost_ref[...])


def _memout_ffn(x, om, wmo, gmpost, gpre, wg, wu, wd, gpost):
    n = x.shape[0]
    tm = _row_tile(n)
    row = pl.BlockSpec((tm, D_MODEL), lambda i: (i, 0))
    return pl.pallas_call(
        _memout_ffn_kernel,
        grid=(n // tm,),
        in_specs=[row, pl.BlockSpec((tm, MEM_W), lambda i: (i, 0)),
                  _const_spec((MEM_W, D_MODEL)), _const_spec((1, D_MODEL)),
                  _const_spec((1, D_MODEL)), _const_spec((D_MODEL, D_FF)),
                  _const_spec((D_MODEL, D_FF)), _const_spec((D_FF, D_MODEL)),
                  _const_spec((1, D_MODEL))],
        out_specs=row,
        out_shape=jax.ShapeDtypeStruct((n, D_MODEL), F32),
        compiler_params=_params("parallel"),
        name="memout_ffn",
    )(x, om, wmo, gmpost, gpre, wg, wu, wd, gpost)


BIG_W = FOX_Q_W + 2 * FOX_KV_W + MLA_Q_LORA + MLA_KV_LORA
SMALL_W = 3 * LANES


def _log_sigmoid(x):
    return jnp.minimum(x, 0.0) - jnp.log1p(jnp.exp(-jnp.abs(x)))


def _mixproj_kernel(x_ref, g_ref, wbig_ref, wsmall_ref, bf_ref, gq_ref, gkv_ref,
                    wqn_ref, wqra_ref, wqrb_ref, wuk_ref, cos_ref, sin_ref,
                    q_ref, k32_ref, v32_ref, ckv32_ref, kr32_ref, lf_ref, c_ref,
                    kT_ref, v16_ref, ckv16_ref, ckvT_ref, krT_ref, lfT_ref, cT_ref,
                    qlat_ref, qrope_ref, carry_ref, *, tiles_per_seq):
    i = pl.program_id(0)
    tm = x_ref.shape[0]
    h = _rms(x_ref[...], g_ref[...]).astype(BF)
    big = _dot(h, wbig_ref[...])
    small = _dot(h, wsmall_ref[...])
    cos = cos_ref[...]
    sin = sin_ref[...]

    q_ref[...] = (big[:, :FOX_Q_W] * FOX_SCALE).astype(BF)
    k = big[:, FOX_Q_W:FOX_Q_W + FOX_KV_W]
    v = big[:, FOX_Q_W + FOX_KV_W:FOX_Q_W + 2 * FOX_KV_W]
    k32_ref[...] = k
    v32_ref[...] = v
    kT_ref[0] = k.T.astype(BF)
    v16_ref[...] = v.astype(BF)

    lf = _log_sigmoid(small[:, 2 * LANES:] + bf_ref[...])
    lf_ref[...] = lf[:, :FOX_HEADS]
    lfT = lf.T
    lfT_ref[0] = lfT[:FOX_HEADS]

    @pl.when(i % tiles_per_seq == 0)
    def _():
        carry_ref[...] = jnp.zeros_like(carry_ref)

    lane = lax.broadcasted_iota(jnp.int32, lfT.shape, 1)
    run = lfT
    sh = 1
    while sh < tm:
        run = run + jnp.where(lane >= sh, pltpu.roll(run, sh, axis=1), 0.0)
        sh *= 2
    run = run + carry_ref[...]
    carry_ref[...] = run[:, tm - 1:tm]
    cT_ref[0] = run[:FOX_HEADS]
    c_ref[...] = run.T[:, :FOX_HEADS]

    cq = big[:, FOX_Q_W + 2 * FOX_KV_W:FOX_Q_W + 2 * FOX_KV_W + MLA_Q_LORA]
    cqn = _rms(cq, gq_ref[...]).astype(BF)
    qn = _dot(cqn, wqn_ref[...]).astype(BF)
    for hd in range(MLA_HEADS):
        ql = _dot(qn[:, hd * MLA_NOPE_DIM:(hd + 1) * MLA_NOPE_DIM], wuk_ref[hd])
        qlat_ref[:, hd * MLA_KV_LORA:(hd + 1) * MLA_KV_LORA] = (ql * MLA_SCALE).astype(BF)
    qr = _dot(cqn, wqra_ref[...]) * cos + _dot(cqn, wqrb_ref[...]) * sin
    qrope_ref[...] = (qr * MLA_SCALE).astype(BF)

    ckv = _rms(big[:, BIG_W - MLA_KV_LORA:], gkv_ref[...])
    ckv32_ref[...] = ckv
    ckv16_ref[...] = ckv.astype(BF)
    ckvT_ref[0] = ckv.T.astype(BF)
    kr = small[:, :LANES] * cos[:, :LANES] + small[:, LANES:2 * LANES] * sin[:, :LANES]
    kr32_ref[...] = kr[:, :MLA_ROPE_DIM]
    krT_ref[0] = kr.T[:MLA_ROPE_DIM].astype(BF)


def _mix_proj(x, seq_len, w, cos_tab, sin_tab):
    n = x.shape[0]
    tm = _row_tile(min(n, seq_len))
    n_seq = n // seq_len
    tps = seq_len // tm
    tab_tiles = cos_tab.shape[0] // tm
    row = lambda width: pl.BlockSpec((tm, width), lambda i: (i, 0))
    colT = lambda height: pl.BlockSpec((1, height, tm), lambda i: (i // tps, 0, i % tps))
    tab = pl.BlockSpec((tm, 2 * LANES), lambda i: (i % tab_tiles, 0))
    sds = jax.ShapeDtypeStruct
    out_shape = (
        sds((n, FOX_Q_W), BF), sds((n, FOX_KV_W), F32), sds((n, FOX_KV_W), F32),
        sds((n, MLA_KV_LORA), F32), sds((n, MLA_ROPE_DIM), F32),
        sds((n, FOX_HEADS), F32), sds((n, FOX_HEADS), F32),
        sds((n_seq, FOX_KV_W, seq_len), BF), sds((n, FOX_KV_W), BF),
        sds((n, MLA_KV_LORA), BF), sds((n_seq, MLA_KV_LORA, seq_len), BF),
        sds((n_seq, MLA_ROPE_DIM, seq_len), BF),
        sds((n_seq, FOX_HEADS, seq_len), F32), sds((n_seq, FOX_HEADS, seq_len), F32),
        sds((n, MLA_HEADS * MLA_KV_LORA), BF), sds((n, MLA_HEADS * MLA_ROPE_DIM), BF),
    )
    out_specs = (
        row(FOX_Q_W), row(FOX_KV_W), row(FOX_KV_W), row(MLA_KV_LORA), row(MLA_ROPE_DIM),
        row(FOX_HEADS), row(FOX_HEADS),
        colT(FOX_KV_W), row(FOX_KV_W), row(MLA_KV_LORA), colT(MLA_KV_LORA),
        colT(MLA_ROPE_DIM), colT(FOX_HEADS), colT(FOX_HEADS),
        row(MLA_HEADS * MLA_KV_LORA), row(MLA_HEADS * MLA_ROPE_DIM),
    )
    in_specs = [
        row(D_MODEL), _const_spec((1, D_MODEL)), _const_spec((D_MODEL, BIG_W)),
        _const_spec((D_MODEL, SMALL_W)), _const_spec((1, LANES)),
        _const_spec((1, MLA_Q_LORA)), _const_spec((1, MLA_KV_LORA)),
        _const_spec((MLA_Q_LORA, MLA_HEADS * MLA_NOPE_DIM)),
        _const_spec((MLA_Q_LORA, MLA_HEADS * MLA_ROPE_DIM)),
        _const_spec((MLA_Q_LORA, MLA_HEADS * MLA_ROPE_DIM)),
        _const_spec((MLA_HEADS, MLA_NOPE_DIM, MLA_KV_LORA)), tab, tab,
    ]
    outs = pl.pallas_call(
        functools.partial(_mixproj_kernel, tiles_per_seq=tps),
        grid=(n // tm,),
        in_specs=in_specs,
        out_specs=out_specs,
        out_shape=out_shape,
        scratch_shapes=[pltpu.VMEM((LANES, 1), F32)],
        compiler_params=_params("arbitrary"),
        name="mix_proj",
    )(x, w["g_mix_pre"], w["w_big"], w["w_small"], w["b_f"], w["g_q_norm"],
      w["g_kv_norm"], w["w_q_nope"], w["w_q_rope_a"], w["w_q_rope_b"], w["w_uk_t"],
      cos_tab, sin_tab)
    names = ("q", "k32", "v32", "ckv32", "kr32", "lf", "c", "kT", "v16", "ckv16",
             "ckvT", "krT", "lfT", "cT", "qlat", "qrope")
    return dict(zip(names, outs))


def _online_update(carry, s, v):
    m, l, acc = carry
    m_new = jnp.maximum(m, jnp.max(s, axis=1, keepdims=True))
    alpha = jnp.exp(m - m_new)
    p = jnp.exp(s - m_new)
    l = alpha * l + jnp.sum(p, axis=1, keepdims=True)
    acc = alpha * acc + _dot(p.astype(BF), v)
    return m_new, l, acc


def _causal_mask(s, q0, k0):
    qpos = q0 + lax.broadcasted_iota(jnp.int32, s.shape, 0)
    kpos = k0 + lax.broadcasted_iota(jnp.int32, s.shape, 1)
    return jnp.where(qpos >= kpos, s, NEG_INF)


def _fox_prompt_kernel(q_ref, c_ref, kT_ref, v_ref, cT_ref, o_ref):
    i = pl.program_id(1)
    tq = q_ref.shape[0]
    ratio = tq // ATT_TK
    for hh in range(FOX_HEADS):
        kvh = hh // FOX_GROUP
        dsl = slice(kvh * FOX_HEAD_DIM, (kvh + 1) * FOX_HEAD_DIM)
        q = q_ref[:, hh * FOX_HEAD_DIM:(hh + 1) * FOX_HEAD_DIM]
        cq = c_ref[:, hh:hh + 1]

        def block(kj, carry, masked):
            k0 = pl.multiple_of(kj * ATT_TK, ATT_TK)
            s = _dot(q, kT_ref[0, dsl, pl.ds(k0, ATT_TK)])
            s = s + (cq - cT_ref[0, hh:hh + 1, pl.ds(k0, ATT_TK)])
            if masked:
                s = _causal_mask(s, i * tq, k0)
            return _online_update(carry, s, v_ref[pl.ds(k0, ATT_TK), dsl])

        carry = (jnp.full((tq, 1), NEG_INF, F32), jnp.zeros((tq, 1), F32),
                 jnp.zeros((tq, FOX_HEAD_DIM), F32))
        carry = lax.fori_loop(0, i * ratio, lambda kj, c: block(kj, c, False), carry)
        for d in range(ratio):
            carry = block(i * ratio + d, carry, True)
        _, l, acc = carry
        o_ref[:, hh * FOX_HEAD_DIM:(hh + 1) * FOX_HEAD_DIM] = (acc / l).astype(BF)


def _fox_prompt(p, n_seq, seq_len):
    tq = ATT_TQ
    nq = seq_len // tq
    return pl.pallas_call(
        _fox_prompt_kernel,
        grid=(n_seq, nq),
        in_specs=[
            pl.BlockSpec((tq, FOX_Q_W), lambda b, i: (b * nq + i, 0)),
            pl.BlockSpec((tq, FOX_HEADS), lambda b, i: (b * nq + i, 0)),
            pl.BlockSpec((1, FOX_KV_W, seq_len), lambda b, i: (b, 0, 0)),
            pl.BlockSpec((seq_len, FOX_KV_W), lambda b, i: (b, 0)),
            pl.BlockSpec((1, FOX_HEADS, seq_len), lambda b, i: (b, 0, 0)),
        ],
        out_specs=pl.BlockSpec((tq, FOX_Q_W), lambda b, i: (b * nq + i, 0)),
        out_shape=jax.ShapeDtypeStruct((n_seq * seq_len, FOX_Q_W), BF),
        compiler_params=_params("parallel", "arbitrary"),
        name="fox_prompt",
    )(p["q"], p["c"], p["kT"], p["v16"], p["cT"])


def _mla_prompt_kernel(ql_ref, qr_ref, ckvT_ref, krT_ref, ckv_ref, o_ref):
    i = pl.program_id(1)
    tq = ql_ref.shape[0]
    ratio = tq // ATT_TK
    for hd in range(MLA_HEADS):
        ql = ql_ref[:, hd * MLA_KV_LORA:(hd + 1) * MLA_KV_LORA]
        qr = qr_ref[:, hd * MLA_ROPE_DIM:(hd + 1) * MLA_ROPE_DIM]

        def block(kj, carry, masked):
            k0 = pl.multiple_of(kj * ATT_TK, ATT_TK)
            s = (_dot(ql, ckvT_ref[0, :, pl.ds(k0, ATT_TK)])
                 + _dot(qr, krT_ref[0, :, pl.ds(k0, ATT_TK)]))
            if masked:
                s = _causal_mask(s, i * tq, k0)
            return _online_update(carry, s, ckv_ref[pl.ds(k0, ATT_TK), :])

        carry = (jnp.full((tq, 1), NEG_INF, F32), jnp.zeros((tq, 1), F32),
                 jnp.zeros((tq, MLA_KV_LORA), F32))
        carry = lax.fori_loop(0, i * ratio, lambda kj, c: block(kj, c, False), carry)
        for d in range(ratio):
            carry = block(i * ratio + d, carry, True)
        _, l, acc = carry
        o_ref[:, hd * MLA_KV_LORA:(hd + 1) * MLA_KV_LORA] = (acc / l).astype(BF)


def _mla_prompt(p, n_seq, seq_len):
    tq = ATT_TQ
    nq = seq_len // tq
    lat_w = MLA_HEADS * MLA_KV_LORA
    return pl.pallas_call(
        _mla_prompt_kernel,
        grid=(n_seq, nq),
        in_specs=[
            pl.BlockSpec((tq, lat_w), lambda b, i: (b * nq + i, 0)),
            pl.BlockSpec((tq, MLA_HEADS * MLA_ROPE_DIM), lambda b, i: (b * nq + i, 0)),
            pl.BlockSpec((1, MLA_KV_LORA, seq_len), lambda b, i: (b, 0, 0)),
            pl.BlockSpec((1, MLA_ROPE_DIM, seq_len), lambda b, i: (b, 0, 0)),
            pl.BlockSpec((seq_len, MLA_KV_LORA), lambda b, i: (b, 0)),
        ],
        out_specs=pl.BlockSpec((tq, lat_w), lambda b, i: (b * nq + i, 0)),
        out_shape=jax.ShapeDtypeStruct((n_seq * seq_len, lat_w), BF),
        compiler_params=_params("parallel", "arbitrary"),
        name="mla_prompt",
    )(p["qlat"], p["qrope"], p["ckvT"], p["krT"], p["ckv16"])


DEC_FOX_ROWS = 4 * FOX_HEADS
DEC_MLA_ROWS = 4 * MLA_HEADS


def _decode_kernel(pt_ref, qbd_ref, lfnew_ref, knew_ref, vnew_ref, qlat_ref, qrope_ref,
                   ckvnew_ref, krnew_ref, kc_ref, vc_ref, ckvc_ref, krc_ref, lfc_ref,
                   of_ref, om_ref,
                   kbuf, vbuf, ckvbuf, krbuf, lfbuf, sems,
                   mf_ref, lf_ref, accf_ref, mm_ref, lm_ref, accm_ref, rcarry_ref,
                   *, n_chunks, n_steps):
    t = pl.program_id(0)
    n_keys = DEC_PAGES * PAGE_SIZE

    def chunk_copies(step, slot):
        b = step // n_chunks
        first = (n_chunks - 1 - step % n_chunks) * DEC_PAGES
        copies = []
        for j in range(DEC_PAGES):
            pid = pt_ref[b, first + j]
            lanes = pl.ds(j * PAGE_SIZE, PAGE_SIZE)
            copies += [
                pltpu.make_async_copy(kc_ref.at[pid], kbuf.at[slot, :, lanes], sems.at[slot, 0]),
                pltpu.make_async_copy(vc_ref.at[pid], vbuf.at[slot, :, lanes], sems.at[slot, 1]),
                pltpu.make_async_copy(ckvc_ref.at[pid], ckvbuf.at[slot, lanes, :], sems.at[slot, 2]),
                pltpu.make_async_copy(krc_ref.at[pid], krbuf.at[slot, :, lanes], sems.at[slot, 3]),
                pltpu.make_async_copy(lfc_ref.at[pid], lfbuf.at[slot, :, lanes], sems.at[slot, 4]),
            ]
        return copies

    slot = t % 2

    @pl.when(t == 0)
    def _():
        for cp in chunk_copies(t, slot):
            cp.start()

    @pl.when(t + 1 < n_steps)
    def _():
        for cp in chunk_copies(t + 1, 1 - slot):
            cp.start()

    @pl.when(t % n_chunks == 0)
    def _():
        mf_ref[...] = jnp.full_like(mf_ref, NEG_INF)
        lf_ref[...] = jnp.zeros_like(lf_ref)
        accf_ref[...] = jnp.zeros_like(accf_ref)
        mm_ref[...] = jnp.full_like(mm_ref, NEG_INF)
        lm_ref[...] = jnp.zeros_like(lm_ref)
        accm_ref[...] = jnp.zeros_like(accm_ref)
        rcarry_ref[...] = jnp.zeros_like(rcarry_ref)

    for cp in chunk_copies(t, slot):
        cp.wait()

    lfnew = lfnew_ref[0]
    parts = [lfnew[0:FOX_HEADS]]
    for u in range(1, 4):
        parts.append(parts[-1] + lfnew[u * FOX_HEADS:(u + 1) * FOX_HEADS])
    ncol = jnp.concatenate(parts, axis=0)
    qbd = qbd_ref[0]

    lfp = lfbuf[slot]
    lane = lax.broadcasted_iota(jnp.int32, lfp.shape, 1)
    run = lfp
    sh = 1
    while sh < n_keys:
        run = run + jnp.where(lane < n_keys - sh, pltpu.roll(run, n_keys - sh, axis=1), 0.0)
        sh *= 2
    later = run - lfp + rcarry_ref[...]
    rcarry_ref[...] = rcarry_ref[...] + run[:, 0:1]
    bias = jnp.concatenate([later] * 4, axis=0) + ncol

    def update(m_ref, l_ref, acc_ref, s, pv_fn):
        m_prev = m_ref[...]
        m_new = jnp.maximum(m_prev, jnp.max(s, axis=1, keepdims=True))
        alpha = jnp.exp(m_prev - m_new)
        p = jnp.exp(s - m_new)
        l_ref[...] = alpha * l_ref[...] + jnp.sum(p, axis=1, keepdims=True)
        acc_ref[...] = alpha * acc_ref[...] + pv_fn(p)
        m_ref[...] = m_new

    s_f = _dot(qbd, kbuf[slot].astype(BF)) + bias
    update(mf_ref, lf_ref, accf_ref, s_f,
           lambda p: _dot_nt(p.astype(BF), vbuf[slot].astype(BF)))

    ckv = ckvbuf[slot].astype(BF)
    qlat = qlat_ref[0]
    qrope = qrope_ref[0]
    s_m = _dot_nt(qlat, ckv) + _dot(qrope, krbuf[slot].astype(BF))
    update(mm_ref, lm_ref, accm_ref, s_m, lambda p: _dot(p.astype(BF), ckv))

    @pl.when(t % n_chunks == n_chunks - 1)
    def _():
        qf = qbd.astype(F32)
        rowf = lax.broadcasted_iota(jnp.int32, (DEC_FOX_ROWS, 1), 0)
        knew = knew_ref[0]
        vnew = vnew_ref[0]
        for u in range(4):
            s = jnp.sum(qf * knew[u:u + 1, :], axis=1, keepdims=True)
            n_u = jnp.concatenate([ncol[u * FOX_HEADS:(u + 1) * FOX_HEADS]] * 4, axis=0)
            s = jnp.where(rowf >= u * FOX_HEADS, s + (ncol - n_u), NEG_INF)
            update(mf_ref, lf_ref, accf_ref, s, lambda p: p * vnew[u:u + 1, :])
        o = accf_ref[...] / lf_ref[...]
        row = lax.broadcasted_iota(jnp.int32, o.shape, 0)
        col = lax.broadcasted_iota(jnp.int32, o.shape, 1)
        o = jnp.where(col // FOX_HEAD_DIM == (row % FOX_HEADS) // FOX_GROUP, o, 0.0)
        of_ref[0] = (o[:, 0:64] + o[:, 64:128]) + (o[:, 128:192] + o[:, 192:256])

        qlf = qlat.astype(F32)
        qrf = qrope.astype(F32)
        rowm = lax.broadcasted_iota(jnp.int32, (DEC_MLA_ROWS, 1), 0)
        ckvnew = ckvnew_ref[0]
        krnew = krnew_ref[0]
        for u in range(4):
            s = (jnp.sum(qlf * ckvnew[u:u + 1, :], axis=1, keepdims=True)
                 + jnp.sum(qrf * krnew[u:u + 1, :], axis=1, keepdims=True))
            s = jnp.where(rowm >= u * MLA_HEADS, s, NEG_INF)
            update(mm_ref, lm_ref, accm_ref, s, lambda p: p * ckvnew[u:u + 1, :])
        om_ref[0] = accm_ref[...] / lm_ref[...]


def _decode_attention(ps, page_table, kc, vc, ckvc, krc, lfc):
    n_b, n_pages = page_table.shape
    n_chunks = n_pages // DEC_PAGES
    n_steps = n_b * n_chunks
    n_keys = DEC_PAGES * PAGE_SIZE

    q = ps["q"].reshape(n_b, 4, FOX_KV_HEADS, FOX_GROUP, FOX_HEAD_DIM)
    eye = jnp.eye(FOX_KV_HEADS, dtype=BF)
    qbd = jnp.einsum("btkgd,kj->btkgjd", q, eye).reshape(n_b, DEC_FOX_ROWS, FOX_KV_W)
    lfnew = ps["lf"].reshape(n_b, DEC_FOX_ROWS, 1)
    knew = ps["k32"].reshape(n_b, 4, FOX_KV_W)
    vnew = ps["v32"].reshape(n_b, 4, FOX_KV_W)
    qlat = ps["qlat"].reshape(n_b, DEC_MLA_ROWS, MLA_KV_LORA)
    qrope = ps["qrope"].reshape(n_b, DEC_MLA_ROWS, MLA_ROPE_DIM)
    ckvnew = ps["ckv32"].reshape(n_b, 4, MLA_KV_LORA)
    krnew = ps["kr32"].reshape(n_b, 4, MLA_ROPE_DIM)

    per_b = lambda r, w: pl.BlockSpec((1, r, w), lambda t, pt: (t // n_chunks, 0, 0))
    hbm = pl.BlockSpec(memory_space=pl.ANY)
    grid_spec = pltpu.PrefetchScalarGridSpec(
        num_scalar_prefetch=1,
        grid=(n_steps,),
        in_specs=[per_b(DEC_FOX_ROWS, FOX_KV_W), per_b(DEC_FOX_ROWS, 1),
                  per_b(4, FOX_KV_W), per_b(4, FOX_KV_W),
                  per_b(DEC_MLA_ROWS, MLA_KV_LORA), per_b(DEC_MLA_ROWS, MLA_ROPE_DIM),
                  per_b(4, MLA_KV_LORA), per_b(4, MLA_ROPE_DIM),
                  hbm, hbm, hbm, hbm, hbm],
        out_specs=[per_b(DEC_FOX_ROWS, FOX_HEAD_DIM), per_b(DEC_MLA_ROWS, MLA_KV_LORA)],
        scratch_shapes=[
            pltpu.VMEM((2, FOX_KV_W, n_keys), F32),
            pltpu.VMEM((2, FOX_KV_W, n_keys), F32),
            pltpu.VMEM((2, n_keys, MLA_KV_LORA), F32),
            pltpu.VMEM((2, MLA_ROPE_DIM, n_keys), F32),
            pltpu.VMEM((2, FOX_HEADS, n_keys), F32),
            pltpu.SemaphoreType.DMA((2, 5)),
            pltpu.VMEM((DEC_FOX_ROWS, 1), F32), pltpu.VMEM((DEC_FOX_ROWS, 1), F32),
            pltpu.VMEM((DEC_FOX_ROWS, FOX_KV_W), F32),
            pltpu.VMEM((DEC_MLA_ROWS, 1), F32), pltpu.VMEM((DEC_MLA_ROWS, 1), F32),
            pltpu.VMEM((DEC_MLA_ROWS, MLA_KV_LORA), F32),
            pltpu.VMEM((FOX_HEADS, 1), F32),
        ],
    )
    o_fox, o_lat = pl.pallas_call(
        functools.partial(_decode_kernel, n_chunks=n_chunks, n_steps=n_steps),
        grid_spec=grid_spec,
        out_shape=(jax.ShapeDtypeStruct((n_b, DEC_FOX_ROWS, FOX_HEAD_DIM), F32),
                   jax.ShapeDtypeStruct((n_b, DEC_MLA_ROWS, MLA_KV_LORA), F32)),
        compiler_params=_params("arbitrary"),
        name="decode_attention",
    )(page_table, qbd, lfnew, knew, vnew, qlat, qrope, ckvnew, krnew, kc, vc, ckvc, krc, lfc)
    return (o_fox.reshape(n_b * 4, FOX_Q_W).astype(BF),
            o_lat.reshape(n_b * 4, MLA_HEADS * MLA_KV_LORA).astype(BF))


def _mixout_kernel(x_ref, of_ref, ol_ref, wuv_ref, wo_ref, gpost_ref, gmpre_ref, wmq_ref,
                   x_out_ref, qm_ref):
    parts = [of_ref[...]]
    for hd in range(MLA_HEADS):
        om = _dot(ol_ref[:, hd * MLA_KV_LORA:(hd + 1) * MLA_KV_LORA], wuv_ref[hd])
        parts.append(om.astype(BF))
    o = jnp.concatenate(parts, axis=1)
    x = x_ref[...] + _rms(_dot(o, wo_ref[...]), gpost_ref[...])
    x_out_ref[...] = x
    hm = _rms(x, gmpre_ref[...]).astype(BF)
    qm_ref[...] = (_dot(hm, wmq_ref[...]) * MEM_SCALE).astype(BF)


def _mix_out(x, o_fox, o_lat, w):
    n = x.shape[0]
    tm = _row_tile(n)
    row = lambda width: pl.BlockSpec((tm, width), lambda i: (i, 0))
    mix_w = FOX_Q_W + MLA_HEADS * MLA_V_DIM
    return pl.pallas_call(
        _mixout_kernel,
        grid=(n // tm,),
        in_specs=[row(D_MODEL), row(FOX_Q_W), row(MLA_HEADS * MLA_KV_LORA),
                  _const_spec((MLA_HEADS, MLA_KV_LORA, MLA_V_DIM)),
                  _const_spec((mix_w, D_MODEL)), _const_spec((1, D_MODEL)),
                  _const_spec((1, D_MODEL)), _const_spec((D_MODEL, MEM_W))],
        out_specs=(row(D_MODEL), row(MEM_W)),
        out_shape=(jax.ShapeDtypeStruct((n, D_MODEL), F32),
                   jax.ShapeDtypeStruct((n, MEM_W), BF)),
        compiler_params=_params("parallel"),
        name="mix_out",
    )(x, o_fox, o_lat, w["w_uv"], w["w_out"], w["g_mix_post"], w["g_mem_pre"], w["w_mem_q"])


def _memkv_kernel(mem_ref, g_ref, w_ref, k_ref, v_ref):
    kv = _dot(_rms(mem_ref[...], g_ref[...]).astype(BF), w_ref[...])
    k_ref[...] = kv[:, :MEM_W]
    v_ref[...] = kv[:, MEM_W:]


def _memory_kv(mem, g, w):
    n = mem.shape[0]
    return pl.pallas_call(
        _memkv_kernel,
        grid=(1,),
        in_specs=[_const_spec((n, D_MODEL)), _const_spec((1, D_MODEL)),
                  _const_spec((D_MODEL, 2 * MEM_W))],
        out_specs=(_const_spec((n, MEM_W)), _const_spec((n, MEM_W))),
        out_shape=(jax.ShapeDtypeStruct((n, MEM_W), F32),) * 2,
        compiler_params=_params("arbitrary"),
        name="memory_kv",
    )(mem, g, w)


def _softmax_rows(s):
    p = jnp.exp(s - jnp.max(s, axis=1, keepdims=True))
    return p, jnp.sum(p, axis=1, keepdims=True)


def _mem_prompt_kernel(q_ref, k_ref, v_ref, o_ref):
    for hd in range(MEM_HEADS):
        sl = slice(hd * MEM_HEAD_DIM, (hd + 1) * MEM_HEAD_DIM)
        p, l = _softmax_rows(_dot_nt(q_ref[:, sl], k_ref[0, :, sl].astype(BF)))
        o = _dot(p.astype(BF), v_ref[0, :, sl].astype(BF))
        o_ref[:, sl] = (o / l).astype(BF)


def _mem_prompt(qm, mk, mv, seq_len):
    n = qm.shape[0]
    tm = _row_tile(seq_len)
    tps = seq_len // tm
    n_mem = mk.shape[1]
    kv = pl.BlockSpec((1, n_mem, MEM_W), lambda i: (i // tps, 0, 0))
    return pl.pallas_call(
        _mem_prompt_kernel,
        grid=(n // tm,),
        in_specs=[pl.BlockSpec((tm, MEM_W), lambda i: (i, 0)), kv, kv],
        out_specs=pl.BlockSpec((tm, MEM_W), lambda i: (i, 0)),
        out_shape=jax.ShapeDtypeStruct((n, MEM_W), BF),
        compiler_params=_params("parallel"),
        name="mem_prompt",
    )(qm, mk, mv)


def _mem_decode_kernel(q_ref, k_ref, v_ref, o_ref):
    rows = q_ref.shape[1]
    row = lax.broadcasted_iota(jnp.int32, (rows, MEM_W), 0)
    col = lax.broadcasted_iota(jnp.int32, (rows, MEM_W), 1)
    own = col // MEM_HEAD_DIM == row % MEM_HEADS
    for b in range(q_ref.shape[0]):
        k = jnp.concatenate([k_ref[b, :, hd, :] for hd in range(MEM_HEADS)], axis=1).astype(BF)
        v = jnp.concatenate([v_ref[b, :, hd, :] for hd in range(MEM_HEADS)], axis=1).astype(BF)
        p, l = _softmax_rows(_dot_nt(q_ref[b], k))
        o = jnp.where(own, _dot(p.astype(BF), v) / l, 0.0)
        o_ref[b] = ((o[:, 0:128] + o[:, 128:256]) + (o[:, 256:384] + o[:, 384:512])).astype(BF)


def _mem_decode(qm, cache_k, cache_v):
    n_b = cache_k.shape[0]
    n_mem = cache_k.shape[1]
    rows = 4 * MEM_HEADS
    q = qm.reshape(n_b, 4, MEM_HEADS, MEM_HEAD_DIM)
    eye = jnp.eye(MEM_HEADS, dtype=BF)
    qbd = jnp.einsum("bthd,hj->bthjd", q, eye).reshape(n_b, rows, MEM_W)
    g = MEM_DEC_BATCH if n_b % MEM_DEC_BATCH == 0 else n_b
    kv = pl.BlockSpec((g, n_mem, MEM_HEADS, MEM_HEAD_DIM), lambda i: (i, 0, 0, 0))
    o = pl.pallas_call(
        _mem_decode_kernel,
        grid=(n_b // g,),
        in_specs=[pl.BlockSpec((g, rows, MEM_W), lambda i: (i, 0, 0)), kv, kv],
        out_specs=pl.BlockSpec((g, rows, MEM_HEAD_DIM), lambda i: (i, 0, 0)),
        out_shape=jax.ShapeDtypeStruct((n_b, rows, MEM_HEAD_DIM), BF),
        compiler_params=_params("parallel"),
        name="mem_decode",
    )(qbd, cache_k, cache_v)
    return o.reshape(n_b * 4, MEM_W)


def _prep_weights(l, g_ffn1_pre, w_ffn1_gu, w_ffn1_down, g_ffn1_post, g_mix_pre, w_in,
                  b_fgate, g_q_norm, w_q_up, g_kv_norm, w_kv_up, w_out, g_mix_post,
                  g_mem_tok, w_mem_kv, g_mem_pre, w_mem_q, w_mem_o, g_mem_post,
                  g_ffn2_pre, w_ffn2_gu, w_ffn2_down, g_ffn2_post):
    row = lambda g: g[l].reshape(1, -1)
    w = {}
    for name, (gpre, wgu, wd, gpost) in {
            "ffn1": (g_ffn1_pre, w_ffn1_gu, w_ffn1_down, g_ffn1_post),
            "ffn2": (g_ffn2_pre, w_ffn2_gu, w_ffn2_down, g_ffn2_post)}.items():
        w[name] = (row(gpre), wgu[l][:, :D_FF].astype(BF), wgu[l][:, D_FF:].astype(BF),
                   wd[l].astype(BF), row(gpost))
    cuts = np.cumsum([FOX_Q_W, FOX_KV_W, FOX_KV_W, FOX_HEADS, MLA_Q_LORA, MLA_KV_LORA]).tolist()
    wq, wk, wv, wf, wcq, wckv, wkr = jnp.split(w_in[l], cuts, axis=1)
    half = MLA_ROPE_DIM // 2
    pad = lambda a: jnp.pad(a, ((0, 0), (0, LANES - a.shape[1])))
    wkr_rot = jnp.concatenate([wkr[:, half:], wkr[:, :half]], axis=1)
    w["g_mix_pre"] = row(g_mix_pre)
    w["w_big"] = jnp.concatenate([wq, wk, wv, wcq, wckv], axis=1).astype(BF)
    w["w_small"] = jnp.concatenate([pad(wkr), pad(wkr_rot), pad(wf)], axis=1).astype(BF)
    w["b_f"] = pad(b_fgate[l].reshape(1, -1))
    w["g_q_norm"] = row(g_q_norm)
    w["g_kv_norm"] = row(g_kv_norm)
    wqu = w_q_up[l].reshape(MLA_Q_LORA, MLA_HEADS, MLA_NOPE_DIM + MLA_ROPE_DIM)
    w["w_q_nope"] = wqu[..., :MLA_NOPE_DIM].reshape(MLA_Q_LORA, -1).astype(BF)
    wqr = wqu[..., MLA_NOPE_DIM:]
    wqr_rot = jnp.concatenate([wqr[..., half:], wqr[..., :half]], axis=-1)
    w["w_q_rope_a"] = wqr.reshape(MLA_Q_LORA, -1).astype(BF)
    w["w_q_rope_b"] = wqr_rot.reshape(MLA_Q_LORA, -1).astype(BF)
    wkv = w_kv_up[l].reshape(MLA_KV_LORA, MLA_HEADS, MLA_NOPE_DIM + MLA_V_DIM)
    w["w_uk_t"] = jnp.transpose(wkv[..., :MLA_NOPE_DIM], (1, 2, 0)).astype(BF)
    w["w_uv"] = jnp.transpose(wkv[..., MLA_NOPE_DIM:], (1, 0, 2)).astype(BF)
    w["w_out"] = w_out[l].astype(BF)
    w["g_mix_post"] = row(g_mix_post)
    w["g_mem_tok"] = row(g_mem_tok)
    w["w_mem_kv"] = w_mem_kv[l].astype(BF)
    w["g_mem_pre"] = row(g_mem_pre)
    w["w_mem_q"] = w_mem_q[l].astype(BF)
    w["w_mem_o"] = w_mem_o[l].astype(BF)
    w["g_mem_post"] = row(g_mem_post)
    return w


def _rope_tables(pos):
    half = MLA_ROPE_DIM // 2
    inv_freq = ROPE_THETA ** (-jnp.arange(half, dtype=F32) / half)
    ang = pos.astype(F32)[:, None] * inv_freq[None, :]
    cos, sin = jnp.cos(ang), jnp.sin(ang)
    cos_t = jnp.tile(jnp.concatenate([cos, cos], axis=1), (1, MLA_HEADS))
    sin_t = jnp.tile(jnp.concatenate([-sin, sin], axis=1), (1, MLA_HEADS))
    return cos_t, sin_t


def kernel(x_prompt, x_sample, mem_prompt, cache_fox_k, cache_fox_v, cache_fox_logf, cache_mla_ckv, cache_mla_krope, cache_mem_k, cache_mem_v, page_table, g_ffn1_pre, w_ffn1_gu, w_ffn1_down, g_ffn1_post, g_mix_pre, w_in, b_fgate, g_q_norm, w_q_up, g_kv_norm, w_kv_up, w_out, g_mix_post, g_mem_tok, w_mem_kv, g_mem_pre, w_mem_q, w_mem_o, g_mem_post, g_ffn2_pre, w_ffn2_gu, w_ffn2_down, g_ffn2_post):
    n_seq, seq_len, _ = x_prompt.shape
    n_dec, dec_seq, _ = x_sample.shape
    depth = w_in.shape[0]
    n_pages = page_table.shape[1]
    past_len = n_pages * PAGE_SIZE
    n_mem = mem_prompt.shape[1]
    assert dec_seq == 4

    xp = x_prompt.reshape(n_seq * seq_len, D_MODEL)
    xs = x_sample.reshape(n_dec * dec_seq, D_MODEL)
    mem = mem_prompt.reshape(n_seq * n_mem, D_MODEL)
    cos_p, sin_p = _rope_tables(jnp.arange(seq_len, dtype=jnp.int32))
    pos_s = past_len + jnp.arange(dec_seq, dtype=jnp.int32)
    cos_s, sin_s = _rope_tables(jnp.tile(pos_s, n_dec))

    outs = {k: [] for k in ("pk", "pv", "plf", "pckv", "pkr", "pmk", "pmv",
                            "sk", "sv", "slf", "sckv", "skr")}
    for l in range(depth):
        w = _prep_weights(l, g_ffn1_pre, w_ffn1_gu, w_ffn1_down, g_ffn1_post, g_mix_pre,
                          w_in, b_fgate, g_q_norm, w_q_up, g_kv_norm, w_kv_up, w_out,
                          g_mix_post, g_mem_tok, w_mem_kv, g_mem_pre, w_mem_q, w_mem_o,
                          g_mem_post, g_ffn2_pre, w_ffn2_gu, w_ffn2_down, g_ffn2_post)

        xp = _ffn_half(xp, *w["ffn1"])
        pp = _mix_proj(xp, seq_len, w, cos_p, sin_p)
        o_fox = _fox_prompt(pp, n_seq, seq_len)
        o_lat = _mla_prompt(pp, n_seq, seq_len)
        xp, qm = _mix_out(xp, o_fox, o_lat, w)
        mk, mv = _memory_kv(mem, w["g_mem_tok"], w["w_mem_kv"])
        om = _mem_prompt(qm, mk.reshape(n_seq, n_mem, MEM_W), mv.reshape(n_seq, n_mem, MEM_W),
                         seq_len)
        xp = _memout_ffn(xp, om, w["w_mem_o"], w["g_mem_post"], *w["ffn2"])
        outs["pk"].append(pp["k32"].reshape(n_seq, seq_len, FOX_KV_HEADS, FOX_HEAD_DIM))
        outs["pv"].append(pp["v32"].reshape(n_seq, seq_len, FOX_KV_HEADS, FOX_HEAD_DIM))
        outs["plf"].append(pp["lf"].reshape(n_seq, seq_len, FOX_HEADS))
        outs["pckv"].append(pp["ckv32"].reshape(n_seq, seq_len, MLA_KV_LORA))
        outs["pkr"].append(pp["kr32"].reshape(n_seq, seq_len, MLA_ROPE_DIM))
        outs["pmk"].append(mk.reshape(n_seq, n_mem, MEM_HEADS, MEM_HEAD_DIM))
        outs["pmv"].append(mv.reshape(n_seq, n_mem, MEM_HEADS, MEM_HEAD_DIM))

        xs = _ffn_half(xs, *w["ffn1"])
        ps = _mix_proj(xs, n_dec * dec_seq, w, cos_s, sin_s)
        n_pool = cache_fox_k.shape[1]
        kc = jnp.transpose(cache_fox_k[l], (0, 2, 3, 1)).reshape(n_pool, FOX_KV_W, PAGE_SIZE)
        vc = jnp.transpose(cache_fox_v[l], (0, 2, 3, 1)).reshape(n_pool, FOX_KV_W, PAGE_SIZE)
        krc = jnp.transpose(cache_mla_krope[l], (0, 2, 1))
        lfc = jnp.transpose(cache_fox_logf[l], (0, 2, 1))
        o_fox, o_lat = _decode_attention(ps, page_table, kc, vc, cache_mla_ckv[l], krc, lfc)
        xs, qm = _mix_out(xs, o_fox, o_lat, w)
        om = _mem_decode(qm, cache_mem_k[l], cache_mem_v[l])
        xs = _memout_ffn(xs, om, w["w_mem_o"], w["g_mem_post"], *w["ffn2"])
        outs["sk"].append(ps["k32"].reshape(n_dec, dec_seq, FOX_KV_HEADS, FOX_HEAD_DIM))
        outs["sv"].append(ps["v32"].reshape(n_dec, dec_seq, FOX_KV_HEADS, FOX_HEAD_DIM))
        outs["slf"].append(ps["lf"].reshape(n_dec, dec_seq, FOX_HEADS))
        outs["sckv"].append(ps["ckv32"].reshape(n_dec, dec_seq, MLA_KV_LORA))
        outs["skr"].append(ps["kr32"].reshape(n_dec, dec_seq, MLA_ROPE_DIM))

    st = {k: jnp.stack(v) for k, v in outs.items()}
    return (xp.reshape(n_seq, seq_len, D_MODEL), xs.reshape(n_dec, dec_seq, D_MODEL),
            st["pk"], st["pv"], st["plf"], st["pckv"], st["pkr"], st["pmk"], st["pmv"],
            st["sk"], st["sv"], st["slf"], st["sckv"], st["skr"])
```

```python
import functools

import numpy as np
import jax
import jax.numpy as jnp
from jax import lax
from jax.experimental import pallas as pl
from jax.experimental.pallas import tpu as pltpu

BF = jnp.bfloat16
F32 = jnp.float32

D_MODEL = 1024
PAGE_SIZE = 128
FOX_HEADS = 8
FOX_KV_HEADS = 4
FOX_GROUP = FOX_HEADS // FOX_KV_HEADS
FOX_HEAD_DIM = 64
FOX_Q_W = FOX_HEADS * FOX_HEAD_DIM
FOX_KV_W = FOX_KV_HEADS * FOX_HEAD_DIM
MLA_HEADS = 4
MLA_Q_LORA = 256
MLA_KV_LORA = 256
MLA_NOPE_DIM = 128
MLA_ROPE_DIM = 64
MLA_V_DIM = 128
ROPE_THETA = 10000.0
MEM_HEADS = 4
MEM_HEAD_DIM = 128
MEM_W = MEM_HEADS * MEM_HEAD_DIM
D_FF = 2816
RMS_EPS = 1e-6
NEG_INF = -1e30

LANES = 128
VMEM_LIMIT_BYTES = 56 * 1024 * 1024

LOG2E = 1.4426950408889634
FOX_SCALE = FOX_HEAD_DIM ** -0.5 * LOG2E
MLA_SCALE = (MLA_NOPE_DIM + MLA_ROPE_DIM) ** -0.5 * LOG2E
MEM_SCALE = MEM_HEAD_DIM ** -0.5

ROW_TILE = 512
FF_CHUNK = 256
ATT_TQ = 256
ATT_TK = 1024
DEC_PAGES = 32
MEM_DEC_BATCH = 8


def _params(*sem):
    return pltpu.CompilerParams(dimension_semantics=sem,
                                vmem_limit_bytes=VMEM_LIMIT_BYTES)


def _rms(x, g):
    return x * lax.rsqrt(jnp.mean(x * x, axis=-1, keepdims=True) + RMS_EPS) * g


def _dot(a, b):
    return jnp.dot(a, b, preferred_element_type=F32)


def _dot_nt(a, b):
    return lax.dot_general(a, b, (((1,), (1,)), ((), ())), preferred_element_type=F32)


def _const_spec(shape):
    zeros = (0,) * len(shape)
    return pl.BlockSpec(shape, lambda *_: zeros)


def _row_tile(n):
    return ROW_TILE if n % ROW_TILE == 0 else n


def _swiglu_half(x, gpre, wg_ref, wu_ref, wd_ref, gpost):
    h = _rms(x, gpre).astype(BF)
    acc = jnp.zeros(x.shape, F32)
    for c in range(D_FF // FF_CHUNK):
        sl = slice(c * FF_CHUNK, (c + 1) * FF_CHUNK)
        g = _dot(h, wg_ref[:, sl])
        u = _dot(h, wu_ref[:, sl])
        a = (g * jax.nn.sigmoid(g) * u).astype(BF)
        acc = acc + _dot(a, wd_ref[sl, :])
    return x + 0.5 * _rms(acc, gpost)


def _ffn_kernel(x_ref, gpre_ref, wg_ref, wu_ref, wd_ref, gpost_ref, o_ref):
    o_ref[...] = _swiglu_half(x_ref[...], gpre_ref[...], wg_ref, wu_ref, wd_ref,
                              gpost_ref[...])


def _ffn_half(x, gpre, wg, wu, wd, gpost):
    n = x.shape[0]
    tm = _row_tile(n)
    row = pl.BlockSpec((tm, D_MODEL), lambda i: (i, 0))
    return pl.pallas_call(
        _ffn_kernel,
        grid=(n // tm,),
        in_specs=[row, _const_spec((1, D_MODEL)), _const_spec((D_MODEL, D_FF)),
                  _const_spec((D_MODEL, D_FF)), _const_spec((D_FF, D_MODEL)),
                  _const_spec((1, D_MODEL))],
        out_specs=row,
        out_shape=jax.ShapeDtypeStruct((n, D_MODEL), F32),
        compiler_params=_params("parallel"),
        name="ffn_half",
    )(x, gpre, wg, wu, wd, gpost)


def _memout_ffn_kernel(x_ref, om_ref, wmo_ref, gmpost_ref, gpre_ref, wg_ref, wu_ref,
                       wd_ref, gpost_ref, o_ref):
    x = x_ref[...] + _rms(_dot(om_ref[...], wmo_ref[...]), gmpost_ref[...])
    o_ref[...] = _swiglu_half(x, gpre_ref[...], wg_ref, wu_ref, wd_ref, gpost_ref[...])


def _memout_ffn(x, om, wmo, gmpost, gpre, wg, wu, wd, gpost):
    n = x.shape[0]
    tm = _row_tile(n)
    row = pl.BlockSpec((tm, D_MODEL), lambda i: (i, 0))
    return pl.pallas_call(
        _memout_ffn_kernel,
        grid=(n // tm,),
        in_specs=[row, pl.BlockSpec((tm, MEM_W), lambda i: (i, 0)),
                  _const_spec((MEM_W, D_MODEL)), _const_spec((1, D_MODEL)),
                  _const_spec((1, D_MODEL)), _const_spec((D_MODEL, D_FF)),
                  _const_spec((D_MODEL, D_FF)), _const_spec((D_FF, D_MODEL)),
                  _const_spec((1, D_MODEL))],
        out_specs=row,
        out_shape=jax.ShapeDtypeStruct((n, D_MODEL), F32),
        compiler_params=_params("parallel"),
        name="memout_ffn",
    )(x, om, wmo, gmpost, gpre, wg, wu, wd, gpost)


BIG_W = FOX_Q_W + 2 * FOX_KV_W + MLA_Q_LORA + MLA_KV_LORA
SMALL_W = 3 * LANES


def _log_sigmoid(x):
    return jnp.minimum(x, 0.0) - jnp.log1p(jnp.exp(-jnp.abs(x)))


def _mixproj_kernel(x_ref, g_ref, wbig_ref, wsmall_ref, bf_ref, gq_ref, gkv_ref,
                    wqn_ref, wqra_ref, wqrb_ref, wuk_ref, cos_ref, sin_ref,
                    q_ref, k32_ref, v32_ref, ckv32_ref, kr32_ref, lf_ref, c_ref,
                    kT_ref, v1_ref, ckv16_ref, ckvT_ref, krT_ref, lfT_ref, cT_ref,
                    qlat_ref, qrope_ref, carry_ref, *, tiles_per_seq):
    i = pl.program_id(0)
    tm = x_ref.shape[0]
    h = _rms(x_ref[...], g_ref[...]).astype(BF)
    big = _dot(h, wbig_ref[...])
    small = _dot(h, wsmall_ref[...])
    cos = cos_ref[...]
    sin = sin_ref[...]

    q_ref[...] = (big[:, :FOX_Q_W] * FOX_SCALE).astype(BF)
    k = big[:, FOX_Q_W:FOX_Q_W + FOX_KV_W]
    v = big[:, FOX_Q_W + FOX_KV_W:FOX_Q_W + 2 * FOX_KV_W]
    k32_ref[...] = k
    v32_ref[...] = v
    kT_ref[0] = k.T.astype(BF)
    one_hot = (lax.broadcasted_iota(jnp.int32, (tm, LANES - FOX_HEAD_DIM), 1) == 0).astype(BF)
    for kvh in range(FOX_KV_HEADS):
        v1_ref[:, kvh * LANES:kvh * LANES + FOX_HEAD_DIM] = (
            v[:, kvh * FOX_HEAD_DIM:(kvh + 1) * FOX_HEAD_DIM].astype(BF))
        v1_ref[:, kvh * LANES + FOX_HEAD_DIM:(kvh + 1) * LANES] = one_hot

    lf = _log_sigmoid(small[:, 2 * LANES:] + bf_ref[...])
    lf_ref[...] = lf[:, :FOX_HEADS]
    lfT = lf.T
    lfT_ref[0] = lfT[:FOX_HEADS]

    @pl.when(i % tiles_per_seq == 0)
    def _():
        carry_ref[...] = jnp.zeros_like(carry_ref)

    lane = lax.broadcasted_iota(jnp.int32, lfT.shape, 1)
    run = lfT
    sh = 1
    while sh < tm:
        run = run + jnp.where(lane >= sh, pltpu.roll(run, sh, axis=1), 0.0)
        sh *= 2
    run = run + carry_ref[...]
    carry_ref[...] = run[:, tm - 1:tm]
    run2 = run * LOG2E
    cT_ref[0] = run2[:FOX_HEADS]
    c_ref[...] = run2.T[:, :FOX_HEADS]

    cq = big[:, FOX_Q_W + 2 * FOX_KV_W:FOX_Q_W + 2 * FOX_KV_W + MLA_Q_LORA]
    cqn = _rms(cq, gq_ref[...]).astype(BF)
    qn = _dot(cqn, wqn_ref[...]).astype(BF)
    for hd in range(MLA_HEADS):
        ql = _dot(qn[:, hd * MLA_NOPE_DIM:(hd + 1) * MLA_NOPE_DIM], wuk_ref[hd])
        qlat_ref[:, hd * MLA_KV_LORA:(hd + 1) * MLA_KV_LORA] = (ql * MLA_SCALE).astype(BF)
    qr = _dot(cqn, wqra_ref[...]) * cos + _dot(cqn, wqrb_ref[...]) * sin
    qrope_ref[...] = (qr * MLA_SCALE).astype(BF)

    ckv = _rms(big[:, BIG_W - MLA_KV_LORA:], gkv_ref[...])
    ckv32_ref[...] = ckv
    ckv16_ref[...] = ckv.astype(BF)
    ckvT_ref[0] = ckv.T.astype(BF)
    kr = small[:, :LANES] * cos[:, :LANES] + small[:, LANES:2 * LANES] * sin[:, :LANES]
    kr32_ref[...] = kr[:, :MLA_ROPE_DIM]
    krT_ref[0] = kr.T[:MLA_ROPE_DIM].astype(BF)


def _mix_proj(x, seq_len, w, cos_tab, sin_tab):
    n = x.shape[0]
    tm = _row_tile(min(n, seq_len))
    n_seq = n // seq_len
    tps = seq_len // tm
    tab_tiles = cos_tab.shape[0] // tm
    row = lambda width: pl.BlockSpec((tm, width), lambda i: (i, 0))
    colT = lambda height: pl.BlockSpec((1, height, tm), lambda i: (i // tps, 0, i % tps))
    tab = pl.BlockSpec((tm, 2 * LANES), lambda i: (i % tab_tiles, 0))
    sds = jax.ShapeDtypeStruct
    out_shape = (
        sds((n, FOX_Q_W), BF), sds((n, FOX_KV_W), F32), sds((n, FOX_KV_W), F32),
        sds((n, MLA_KV_LORA), F32), sds((n, MLA_ROPE_DIM), F32),
        sds((n, FOX_HEADS), F32), sds((n, FOX_HEADS), F32),
        sds((n_seq, FOX_KV_W, seq_len), BF), sds((n, FOX_KV_HEADS * LANES), BF),
        sds((n, MLA_KV_LORA), BF), sds((n_seq, MLA_KV_LORA, seq_len), BF),
        sds((n_seq, MLA_ROPE_DIM, seq_len), BF),
        sds((n_seq, FOX_HEADS, seq_len), F32), sds((n_seq, FOX_HEADS, seq_len), F32),
        sds((n, MLA_HEADS * MLA_KV_LORA), BF), sds((n, MLA_HEADS * MLA_ROPE_DIM), BF),
    )
    out_specs = (
        row(FOX_Q_W), row(FOX_KV_W), row(FOX_KV_W), row(MLA_KV_LORA), row(MLA_ROPE_DIM),
        row(FOX_HEADS), row(FOX_HEADS),
        colT(FOX_KV_W), row(FOX_KV_HEADS * LANES), row(MLA_KV_LORA), colT(MLA_KV_LORA),
        colT(MLA_ROPE_DIM), colT(FOX_HEADS), colT(FOX_HEADS),
        row(MLA_HEADS * MLA_KV_LORA), row(MLA_HEADS * MLA_ROPE_DIM),
    )
    in_specs = [
        row(D_MODEL), _const_spec((1, D_MODEL)), _const_spec((D_MODEL, BIG_W)),
        _const_spec((D_MODEL, SMALL_W)), _const_spec((1, LANES)),
        _const_spec((1, MLA_Q_LORA)), _const_spec((1, MLA_KV_LORA)),
        _const_spec((MLA_Q_LORA, MLA_HEADS * MLA_NOPE_DIM)),
        _const_spec((MLA_Q_LORA, MLA_HEADS * MLA_ROPE_DIM)),
        _const_spec((MLA_Q_LORA, MLA_HEADS * MLA_ROPE_DIM)),
        _const_spec((MLA_HEADS, MLA_NOPE_DIM, MLA_KV_LORA)), tab, tab,
    ]
    outs = pl.pallas_call(
        functools.partial(_mixproj_kernel, tiles_per_seq=tps),
        grid=(n // tm,),
        in_specs=in_specs,
        out_specs=out_specs,
        out_shape=out_shape,
        scratch_shapes=[pltpu.VMEM((LANES, 1), F32)],
        compiler_params=_params("arbitrary"),
        name="mix_proj",
    )(x, w["g_mix_pre"], w["w_big"], w["w_small"], w["b_f"], w["g_q_norm"],
      w["g_kv_norm"], w["w_q_nope"], w["w_q_rope_a"], w["w_q_rope_b"], w["w_uk_t"],
      cos_tab, sin_tab)
    names = ("q", "k32", "v32", "ckv32", "kr32", "lf", "c", "kT", "v1", "ckv16",
             "ckvT", "krT", "lfT", "cT", "qlat", "qrope")
    return dict(zip(names, outs))


def _online_update(m_ref, l_ref, acc_ref, h, z, v, row_bias=None):
    m_prev = m_ref[h]
    m_z = jnp.max(z, axis=1, keepdims=True)
    if row_bias is None:
        m_new = jnp.maximum(m_prev, m_z)
        shift = m_new
    else:
        m_new = jnp.maximum(m_prev, m_z + row_bias)
        shift = m_new - row_bias
    alpha = jnp.exp2(m_prev - m_new)
    p = jnp.exp2(z - shift)
    if l_ref is not None:
        l_ref[h] = alpha * l_ref[h] + jnp.sum(p, axis=1, keepdims=True)
    acc_ref[h] = alpha * acc_ref[h] + _dot(p.astype(BF), v)
    m_ref[h] = m_new


def _init_state(m_ref, l_ref, acc_ref):
    m_ref[...] = jnp.full_like(m_ref, NEG_INF)
    if l_ref is not None:
        l_ref[...] = jnp.zeros_like(l_ref)
    acc_ref[...] = jnp.zeros_like(acc_ref)


def _causal_keep(shape, q0, k0):
    qpos = q0 + lax.broadcasted_iota(jnp.int32, shape, 0)
    kpos = k0 + lax.broadcasted_iota(jnp.int32, shape, 1)
    return qpos >= kpos


def _sweep_key_blocks(block, i, tq):
    assert ATT_TK % tq == 0
    n_full = (i * tq) // ATT_TK

    def body(kj, carry):
        block(pl.multiple_of(kj * ATT_TK, ATT_TK), None)
        return carry

    lax.fori_loop(0, n_full, body, 0)
    k0 = pl.multiple_of(n_full * ATT_TK, ATT_TK)
    block(k0, _causal_keep((tq, ATT_TK), i * tq, k0))


def _fox_prompt_kernel(q_ref, c_ref, kT_ref, v1_ref, cT_ref, o_ref, qs_ref, m_ref, acc_ref):
    i = pl.program_id(1)
    tq = q_ref.shape[0]
    for hh in range(FOX_HEADS):
        qs_ref[hh] = q_ref[:, hh * FOX_HEAD_DIM:(hh + 1) * FOX_HEAD_DIM]
    _init_state(m_ref, None, acc_ref)

    def block(k0, keep):
        keys = pl.ds(k0, ATT_TK)

        def scores(hh):
            kvh = hh // FOX_GROUP
            kT = kT_ref[0, kvh * FOX_HEAD_DIM:(kvh + 1) * FOX_HEAD_DIM, keys]
            return _dot(qs_ref[hh], kT) - cT_ref[0, hh:hh + 1, keys]

        z = scores(0)
        for hh in range(FOX_HEADS):
            z_next = scores(hh + 1) if hh + 1 < FOX_HEADS else None
            if keep is not None:
                z = jnp.where(keep, z, NEG_INF)
            kvh = hh // FOX_GROUP
            _online_update(m_ref, None, acc_ref, hh, z, v1_ref[keys, kvh * LANES:(kvh + 1) * LANES],
                           row_bias=c_ref[:, hh:hh + 1])
            z = z_next

    _sweep_key_blocks(block, i, tq)
    for hh in range(FOX_HEADS):
        acc = acc_ref[hh]
        o = acc[:, :FOX_HEAD_DIM] / acc[:, FOX_HEAD_DIM:FOX_HEAD_DIM + 1]
        o_ref[:, hh * FOX_HEAD_DIM:(hh + 1) * FOX_HEAD_DIM] = o.astype(BF)


def _fox_prompt(p, n_seq, seq_len):
    tq = ATT_TQ
    nq = seq_len // tq
    return pl.pallas_call(
        _fox_prompt_kernel,
        grid=(n_seq, nq),
        in_specs=[
            pl.BlockSpec((tq, FOX_Q_W), lambda b, i: (b * nq + i, 0)),
            pl.BlockSpec((tq, FOX_HEADS), lambda b, i: (b * nq + i, 0)),
            pl.BlockSpec((1, FOX_KV_W, seq_len), lambda b, i: (b, 0, 0)),
            pl.BlockSpec((seq_len, FOX_KV_HEADS * LANES), lambda b, i: (b, 0)),
            pl.BlockSpec((1, FOX_HEADS, seq_len), lambda b, i: (b, 0, 0)),
        ],
        out_specs=pl.BlockSpec((tq, FOX_Q_W), lambda b, i: (b * nq + i, 0)),
        out_shape=jax.ShapeDtypeStruct((n_seq * seq_len, FOX_Q_W), BF),
        scratch_shapes=[pltpu.VMEM((FOX_HEADS, tq, FOX_HEAD_DIM), BF),
                        pltpu.VMEM((FOX_HEADS, tq, 1), F32),
                        pltpu.VMEM((FOX_HEADS, tq, LANES), F32)],
        compiler_params=_params("parallel", "arbitrary"),
        name="fox_prompt",
    )(p["q"], p["c"], p["kT"], p["v1"], p["cT"])


def _mla_prompt_kernel(ql_ref, qr_ref, ckvT_ref, krT_ref, ckv_ref, o_ref, qrs_ref, m_ref, l_ref,
                       acc_ref):
    i = pl.program_id(1)
    tq = ql_ref.shape[0]
    for hd in range(MLA_HEADS):
        qrs_ref[hd] = qr_ref[:, hd * MLA_ROPE_DIM:(hd + 1) * MLA_ROPE_DIM]
    _init_state(m_ref, l_ref, acc_ref)

    def block(k0, keep):
        keys = pl.ds(k0, ATT_TK)
        ckvT = ckvT_ref[0, :, keys]
        krT = krT_ref[0, :, keys]
        ckv = ckv_ref[keys, :]

        def scores(hd):
            return (_dot(ql_ref[:, hd * MLA_KV_LORA:(hd + 1) * MLA_KV_LORA], ckvT)
                    + _dot(qrs_ref[hd], krT))

        z = scores(0)
        for hd in range(MLA_HEADS):
            z_next = scores(hd + 1) if hd + 1 < MLA_HEADS else None
            if keep is not None:
                z = jnp.where(keep, z, NEG_INF)
            _online_update(m_ref, l_ref, acc_ref, hd, z, ckv)
            z = z_next

    _sweep_key_blocks(block, i, tq)
    for hd in range(MLA_HEADS):
        o_ref[:, hd * MLA_KV_LORA:(hd + 1) * MLA_KV_LORA] = (acc_ref[hd] / l_ref[hd]).astype(BF)


def _mla_prompt(p, n_seq, seq_len):
    tq = ATT_TQ
    nq = seq_len // tq
    lat_w = MLA_HEADS * MLA_KV_LORA
    return pl.pallas_call(
        _mla_prompt_kernel,
        grid=(n_seq, nq),
        in_specs=[
            pl.BlockSpec((tq, lat_w), lambda b, i: (b * nq + i, 0)),
            pl.BlockSpec((tq, MLA_HEADS * MLA_ROPE_DIM), lambda b, i: (b * nq + i, 0)),
            pl.BlockSpec((1, MLA_KV_LORA, seq_len), lambda b, i: (b, 0, 0)),
            pl.BlockSpec((1, MLA_ROPE_DIM, seq_len), lambda b, i: (b, 0, 0)),
            pl.BlockSpec((seq_len, MLA_KV_LORA), lambda b, i: (b, 0)),
        ],
        out_specs=pl.BlockSpec((tq, lat_w), lambda b, i: (b * nq + i, 0)),
        out_shape=jax.ShapeDtypeStruct((n_seq * seq_len, lat_w), BF),
        scratch_shapes=[pltpu.VMEM((MLA_HEADS, tq, MLA_ROPE_DIM), BF),
                        pltpu.VMEM((MLA_HEADS, tq, 1), F32),
                        pltpu.VMEM((MLA_HEADS, tq, 1), F32),
                        pltpu.VMEM((MLA_HEADS, tq, MLA_KV_LORA), F32)],
        compiler_params=_params("parallel", "arbitrary"),
        name="mla_prompt",
    )(p["qlat"], p["qrope"], p["ckvT"], p["krT"], p["ckv16"])


DEC_FOX_ROWS = 4 * FOX_HEADS
DEC_MLA_ROWS = 4 * MLA_HEADS


def _decode_kernel(pt_ref, qbd_ref, lfnew_ref, knew_ref, vnew_ref, qlat_ref, qrope_ref,
                   ckvnew_ref, krnew_ref, kc_ref, vc_ref, ckvc_ref, krc_ref, lfc_ref,
                   of_ref, om_ref,
                   kbuf, vbuf, ckvbuf, krbuf, lfbuf, sems,
                   mf_ref, lf_ref, accf_ref, mm_ref, lm_ref, accm_ref, rcarry_ref,
                   *, n_chunks, n_steps):
    t = pl.program_id(0)
    n_keys = DEC_PAGES * PAGE_SIZE

    def chunk_copies(step, slot):
        b = step // n_chunks
        first = (n_chunks - 1 - step % n_chunks) * DEC_PAGES
        copies = []
        for j in range(DEC_PAGES):
            pid = pt_ref[b, first + j]
            lanes = pl.ds(j * PAGE_SIZE, PAGE_SIZE)
            copies += [
                pltpu.make_async_copy(kc_ref.at[pid], kbuf.at[slot, :, lanes], sems.at[slot, 0]),
                pltpu.make_async_copy(vc_ref.at[pid], vbuf.at[slot, :, lanes], sems.at[slot, 1]),
                pltpu.make_async_copy(ckvc_ref.at[pid], ckvbuf.at[slot, lanes, :], sems.at[slot, 2]),
                pltpu.make_async_copy(krc_ref.at[pid], krbuf.at[slot, :, lanes], sems.at[slot, 3]),
                pltpu.make_async_copy(lfc_ref.at[pid], lfbuf.at[slot, :, lanes], sems.at[slot, 4]),
            ]
        return copies

    slot = t % 2

    @pl.when(t == 0)
    def _():
        for cp in chunk_copies(t, slot):
            cp.start()

    @pl.when(t + 1 < n_steps)
    def _():
        for cp in chunk_copies(t + 1, 1 - slot):
            cp.start()

    @pl.when(t % n_chunks == 0)
    def _():
        mf_ref[...] = jnp.full_like(mf_ref, NEG_INF)
        lf_ref[...] = jnp.zeros_like(lf_ref)
        accf_ref[...] = jnp.zeros_like(accf_ref)
        mm_ref[...] = jnp.full_like(mm_ref, NEG_INF)
        lm_ref[...] = jnp.zeros_like(lm_ref)
        accm_ref[...] = jnp.zeros_like(accm_ref)
        rcarry_ref[...] = jnp.zeros_like(rcarry_ref)

    for cp in chunk_copies(t, slot):
        cp.wait()

    lfnew = lfnew_ref[0]
    parts = [lfnew[0:FOX_HEADS]]
    for u in range(1, 4):
        parts.append(parts[-1] + lfnew[u * FOX_HEADS:(u + 1) * FOX_HEADS])
    ncol = jnp.concatenate(parts, axis=0) * LOG2E
    qbd = qbd_ref[0]

    lfp = lfbuf[slot]
    lane = lax.broadcasted_iota(jnp.int32, lfp.shape, 1)
    run = lfp
    sh = 1
    while sh < n_keys:
        run = run + jnp.where(lane < n_keys - sh, pltpu.roll(run, n_keys - sh, axis=1), 0.0)
        sh *= 2
    later = run - lfp + rcarry_ref[...]
    rcarry_ref[...] = rcarry_ref[...] + run[:, 0:1]
    bias = jnp.concatenate([later * LOG2E] * 4, axis=0) + ncol

    def update(m_ref, l_ref, acc_ref, s, pv_fn):
        m_prev = m_ref[...]
        m_new = jnp.maximum(m_prev, jnp.max(s, axis=1, keepdims=True))
        alpha = jnp.exp2(m_prev - m_new)
        p = jnp.exp2(s - m_new)
        l_ref[...] = alpha * l_ref[...] + jnp.sum(p, axis=1, keepdims=True)
        acc_ref[...] = alpha * acc_ref[...] + pv_fn(p)
        m_ref[...] = m_new

    s_f = _dot(qbd, kbuf[slot].astype(BF)) + bias
    update(mf_ref, lf_ref, accf_ref, s_f,
           lambda p: _dot_nt(p.astype(BF), vbuf[slot].astype(BF)))

    ckv = ckvbuf[slot].astype(BF)
    qlat = qlat_ref[0]
    qrope = qrope_ref[0]
    s_m = _dot_nt(qlat, ckv) + _dot(qrope, krbuf[slot].astype(BF))
    update(mm_ref, lm_ref, accm_ref, s_m, lambda p: _dot(p.astype(BF), ckv))

    @pl.when(t % n_chunks == n_chunks - 1)
    def _():
        qf = qbd.astype(F32)
        rowf = lax.broadcasted_iota(jnp.int32, (DEC_FOX_ROWS, 1), 0)
        knew = knew_ref[0]
        vnew = vnew_ref[0]
        for u in range(4):
            s = jnp.sum(qf * knew[u:u + 1, :], axis=1, keepdims=True)
            n_u = jnp.concatenate([ncol[u * FOX_HEADS:(u + 1) * FOX_HEADS]] * 4, axis=0)
            s = jnp.where(rowf >= u * FOX_HEADS, s + (ncol - n_u), NEG_INF)
            update(mf_ref, lf_ref, accf_ref, s, lambda p: p * vnew[u:u + 1, :])
        o = accf_ref[...] / lf_ref[...]
        row = lax.broadcasted_iota(jnp.int32, o.shape, 0)
        col = lax.broadcasted_iota(jnp.int32, o.shape, 1)
        o = jnp.where(col // FOX_HEAD_DIM == (row % FOX_HEADS) // FOX_GROUP, o, 0.0)
        of_ref[0] = (o[:, 0:64] + o[:, 64:128]) + (o[:, 128:192] + o[:, 192:256])

        qlf = qlat.astype(F32)
        qrf = qrope.astype(F32)
        rowm = lax.broadcasted_iota(jnp.int32, (DEC_MLA_ROWS, 1), 0)
        ckvnew = ckvnew_ref[0]
        krnew = krnew_ref[0]
        for u in range(4):
            s = (jnp.sum(qlf * ckvnew[u:u + 1, :], axis=1, keepdims=True)
                 + jnp.sum(qrf * krnew[u:u + 1, :], axis=1, keepdims=True))
            s = jnp.where(rowm >= u * MLA_HEADS, s, NEG_INF)
            update(mm_ref, lm_ref, accm_ref, s, lambda p: p * ckvnew[u:u + 1, :])
        om_ref[0] = accm_ref[...] / lm_ref[...]


def _decode_attention(ps, page_table, kc, vc, ckvc, krc, lfc):
    n_b, n_pages = page_table.shape
    n_chunks = n_pages // DEC_PAGES
    n_steps = n_b * n_chunks
    n_keys = DEC_PAGES * PAGE_SIZE

    q = ps["q"].reshape(n_b, 4, FOX_KV_HEADS, FOX_GROUP, FOX_HEAD_DIM)
    eye = jnp.eye(FOX_KV_HEADS, dtype=BF)
    qbd = jnp.einsum("btkgd,kj->btkgjd", q, eye).reshape(n_b, DEC_FOX_ROWS, FOX_KV_W)
    lfnew = ps["lf"].reshape(n_b, DEC_FOX_ROWS, 1)
    knew = ps["k32"].reshape(n_b, 4, FOX_KV_W)
    vnew = ps["v32"].reshape(n_b, 4, FOX_KV_W)
    qlat = ps["qlat"].reshape(n_b, DEC_MLA_ROWS, MLA_KV_LORA)
    qrope = ps["qrope"].reshape(n_b, DEC_MLA_ROWS, MLA_ROPE_DIM)
    ckvnew = ps["ckv32"].reshape(n_b, 4, MLA_KV_LORA)
    krnew = ps["kr32"].reshape(n_b, 4, MLA_ROPE_DIM)

    per_b = lambda r, w: pl.BlockSpec((1, r, w), lambda t, pt: (t // n_chunks, 0, 0))
    hbm = pl.BlockSpec(memory_space=pl.ANY)
    grid_spec = pltpu.PrefetchScalarGridSpec(
        num_scalar_prefetch=1,
        grid=(n_steps,),
        in_specs=[per_b(DEC_FOX_ROWS, FOX_KV_W), per_b(DEC_FOX_ROWS, 1),
                  per_b(4, FOX_KV_W), per_b(4, FOX_KV_W),
                  per_b(DEC_MLA_ROWS, MLA_KV_LORA), per_b(DEC_MLA_ROWS, MLA_ROPE_DIM),
                  per_b(4, MLA_KV_LORA), per_b(4, MLA_ROPE_DIM),
                  hbm, hbm, hbm, hbm, hbm],
        out_specs=[per_b(DEC_FOX_ROWS, FOX_HEAD_DIM), per_b(DEC_MLA_ROWS, MLA_KV_LORA)],
        scratch_shapes=[
            pltpu.VMEM((2, FOX_KV_W, n_keys), F32),
            pltpu.VMEM((2, FOX_KV_W, n_keys), F32),
            pltpu.VMEM((2, n_keys, MLA_KV_LORA), F32),
            pltpu.VMEM((2, MLA_ROPE_DIM, n_keys), F32),
            pltpu.VMEM((2, FOX_HEADS, n_keys), F32),
            pltpu.SemaphoreType.DMA((2, 5)),
            pltpu.VMEM((DEC_FOX_ROWS, 1), F32), pltpu.VMEM((DEC_FOX_ROWS, 1), F32),
            pltpu.VMEM((DEC_FOX_ROWS, FOX_KV_W), F32),
            pltpu.VMEM((DEC_MLA_ROWS, 1), F32), pltpu.VMEM((DEC_MLA_ROWS, 1), F32),
            pltpu.VMEM((DEC_MLA_ROWS, MLA_KV_LORA), F32),
            pltpu.VMEM((FOX_HEADS, 1), F32),
        ],
    )
    o_fox, o_lat = pl.pallas_call(
        functools.partial(_decode_kernel, n_chunks=n_chunks, n_steps=n_steps),
        grid_spec=grid_spec,
        out_shape=(jax.ShapeDtypeStruct((n_b, DEC_FOX_ROWS, FOX_HEAD_DIM), F32),
                   jax.ShapeDtypeStruct((n_b, DEC_MLA_ROWS, MLA_KV_LORA), F32)),
        compiler_params=_params("arbitrary"),
        name="decode_attention",
    )(page_table, qbd, lfnew, knew, vnew, qlat, qrope, ckvnew, krnew, kc, vc, ckvc, krc, lfc)
    return (o_fox.reshape(n_b * 4, FOX_Q_W).astype(BF),
            o_lat.reshape(n_b * 4, MLA_HEADS * MLA_KV_LORA).astype(BF))


def _mixout_kernel(x_ref, of_ref, ol_ref, wuv_ref, wo_ref, gpost_ref, gmpre_ref, wmq_ref,
                   x_out_ref, qm_ref):
    parts = [of_ref[...]]
    for hd in range(MLA_HEADS):
        om = _dot(ol_ref[:, hd * MLA_KV_LORA:(hd + 1) * MLA_KV_LORA], wuv_ref[hd])
        parts.append(om.astype(BF))
    o = jnp.concatenate(parts, axis=1)
    x = x_ref[...] + _rms(_dot(o, wo_ref[...]), gpost_ref[...])
    x_out_ref[...] = x
    hm = _rms(x, gmpre_ref[...]).astype(BF)
    qm_ref[...] = (_dot(hm, wmq_ref[...]) * MEM_SCALE).astype(BF)


def _mix_out(x, o_fox, o_lat, w):
    n = x.shape[0]
    tm = _row_tile(n)
    row = lambda width: pl.BlockSpec((tm, width), lambda i: (i, 0))
    mix_w = FOX_Q_W + MLA_HEADS * MLA_V_DIM
    return pl.pallas_call(
        _mixout_kernel,
        grid=(n // tm,),
        in_specs=[row(D_MODEL), row(FOX_Q_W), row(MLA_HEADS * MLA_KV_LORA),
                  _const_spec((MLA_HEADS, MLA_KV_LORA, MLA_V_DIM)),
                  _const_spec((mix_w, D_MODEL)), _const_spec((1, D_MODEL)),
                  _const_spec((1, D_MODEL)), _const_spec((D_MODEL, MEM_W))],
        out_specs=(row(D_MODEL), row(MEM_W)),
        out_shape=(jax.ShapeDtypeStruct((n, D_MODEL), F32),
                   jax.ShapeDtypeStruct((n, MEM_W), BF)),
        compiler_params=_params("parallel"),
        name="mix_out",
    )(x, o_fox, o_lat, w["w_uv"], w["w_out"], w["g_mix_post"], w["g_mem_pre"], w["w_mem_q"])


def _memkv_kernel(mem_ref, g_ref, w_ref, k_ref, v_ref):
    kv = _dot(_rms(mem_ref[...], g_ref[...]).astype(BF), w_ref[...])
    k_ref[...] = kv[:, :MEM_W]
    v_ref[...] = kv[:, MEM_W:]


def _memory_kv(mem, g, w):
    n = mem.shape[0]
    return pl.pallas_call(
        _memkv_kernel,
        grid=(1,),
        in_specs=[_const_spec((n, D_MODEL)), _const_spec((1, D_MODEL)),
                  _const_spec((D_MODEL, 2 * MEM_W))],
        out_specs=(_const_spec((n, MEM_W)), _const_spec((n, MEM_W))),
        out_shape=(jax.ShapeDtypeStruct((n, MEM_W), F32),) * 2,
        compiler_params=_params("arbitrary"),
        name="memory_kv",
    )(mem, g, w)


def _softmax_rows(s):
    p = jnp.exp(s - jnp.max(s, axis=1, keepdims=True))
    return p, jnp.sum(p, axis=1, keepdims=True)


def _mem_prompt_kernel(q_ref, k_ref, v_ref, o_ref):
    for hd in range(MEM_HEADS):
        sl = slice(hd * MEM_HEAD_DIM, (hd + 1) * MEM_HEAD_DIM)
        p, l = _softmax_rows(_dot_nt(q_ref[:, sl], k_ref[0, :, sl].astype(BF)))
        o = _dot(p.astype(BF), v_ref[0, :, sl].astype(BF))
        o_ref[:, sl] = (o / l).astype(BF)


def _mem_prompt(qm, mk, mv, seq_len):
    n = qm.shape[0]
    tm = _row_tile(seq_len)
    tps = seq_len // tm
    n_mem = mk.shape[1]
    kv = pl.BlockSpec((1, n_mem, MEM_W), lambda i: (i // tps, 0, 0))
    return pl.pallas_call(
        _mem_prompt_kernel,
        grid=(n // tm,),
        in_specs=[pl.BlockSpec((tm, MEM_W), lambda i: (i, 0)), kv, kv],
        out_specs=pl.BlockSpec((tm, MEM_W), lambda i: (i, 0)),
        out_shape=jax.ShapeDtypeStruct((n, MEM_W), BF),
        compiler_params=_params("parallel"),
        name="mem_prompt",
    )(qm, mk, mv)


def _mem_decode_kernel(q_ref, k_ref, v_ref, o_ref):
    rows = q_ref.shape[1]
    row = lax.broadcasted_iota(jnp.int32, (rows, MEM_W), 0)
    col = lax.broadcasted_iota(jnp.int32, (rows, MEM_W), 1)
    own = col // MEM_HEAD_DIM == row % MEM_HEADS
    for b in range(q_ref.shape[0]):
        k = jnp.concatenate([k_ref[b, :, hd, :] for hd in range(MEM_HEADS)], axis=1).astype(BF)
        v = jnp.concatenate([v_ref[b, :, hd, :] for hd in range(MEM_HEADS)], axis=1).astype(BF)
        p, l = _softmax_rows(_dot_nt(q_ref[b], k))
        o = jnp.where(own, _dot(p.astype(BF), v) / l, 0.0)
        o_ref[b] = ((o[:, 0:128] + o[:, 128:256]) + (o[:, 256:384] + o[:, 384:512])).astype(BF)


def _mem_decode(qm, cache_k, cache_v):
    n_b = cache_k.shape[0]
    n_mem = cache_k.shape[1]
    rows = 4 * MEM_HEADS
    q = qm.reshape(n_b, 4, MEM_HEADS, MEM_HEAD_DIM)
    eye = jnp.eye(MEM_HEADS, dtype=BF)
    qbd = jnp.einsum("bthd,hj->bthjd", q, eye).reshape(n_b, rows, MEM_W)
    g = MEM_DEC_BATCH if n_b % MEM_DEC_BATCH == 0 else n_b
    kv = pl.BlockSpec((g, n_mem, MEM_HEADS, MEM_HEAD_DIM), lambda i: (i, 0, 0, 0))
    o = pl.pallas_call(
        _mem_decode_kernel,
        grid=(n_b // g,),
        in_specs=[pl.BlockSpec((g, rows, MEM_W), lambda i: (i, 0, 0)), kv, kv],
        out_specs=pl.BlockSpec((g, rows, MEM_HEAD_DIM), lambda i: (i, 0, 0)),
        out_shape=jax.ShapeDtypeStruct((n_b, rows, MEM_HEAD_DIM), BF),
        compiler_params=_params("parallel"),
        name="mem_decode",
    )(qbd, cache_k, cache_v)
    return o.reshape(n_b * 4, MEM_W)


def _prep_weights(l, g_ffn1_pre, w_ffn1_gu, w_ffn1_down, g_ffn1_post, g_mix_pre, w_in,
                  b_fgate, g_q_norm, w_q_up, g_kv_norm, w_kv_up, w_out, g_mix_post,
                  g_mem_tok, w_mem_kv, g_mem_pre, w_mem_q, w_mem_o, g_mem_post,
                  g_ffn2_pre, w_ffn2_gu, w_ffn2_down, g_ffn2_post):
    row = lambda g: g[l].reshape(1, -1)
    w = {}
    for name, (gpre, wgu, wd, gpost) in {
            "ffn1": (g_ffn1_pre, w_ffn1_gu, w_ffn1_down, g_ffn1_post),
            "ffn2": (g_ffn2_pre, w_ffn2_gu, w_ffn2_down, g_ffn2_post)}.items():
        w[name] = (row(gpre), wgu[l][:, :D_FF].astype(BF), wgu[l][:, D_FF:].astype(BF),
                   wd[l].astype(BF), row(gpost))
    cuts = np.cumsum([FOX_Q_W, FOX_KV_W, FOX_KV_W, FOX_HEADS, MLA_Q_LORA, MLA_KV_LORA]).tolist()
    wq, wk, wv, wf, wcq, wckv, wkr = jnp.split(w_in[l], cuts, axis=1)
    half = MLA_ROPE_DIM // 2
    pad = lambda a: jnp.pad(a, ((0, 0), (0, LANES - a.shape[1])))
    wkr_rot = jnp.concatenate([wkr[:, half:], wkr[:, :half]], axis=1)
    w["g_mix_pre"] = row(g_mix_pre)
    w["w_big"] = jnp.concatenate([wq, wk, wv, wcq, wckv], axis=1).astype(BF)
    w["w_small"] = jnp.concatenate([pad(wkr), pad(wkr_rot), pad(wf)], axis=1).astype(BF)
    w["b_f"] = pad(b_fgate[l].reshape(1, -1))
    w["g_q_norm"] = row(g_q_norm)
    w["g_kv_norm"] = row(g_kv_norm)
    wqu = w_q_up[l].reshape(MLA_Q_LORA, MLA_HEADS, MLA_NOPE_DIM + MLA_ROPE_DIM)
    w["w_q_nope"] = wqu[..., :MLA_NOPE_DIM].reshape(MLA_Q_LORA, -1).astype(BF)
    wqr = wqu[..., MLA_NOPE_DIM:]
    wqr_rot = jnp.concatenate([wqr[..., half:], wqr[..., :half]], axis=-1)
    w["w_q_rope_a"] = wqr.reshape(MLA_Q_LORA, -1).astype(BF)
    w["w_q_rope_b"] = wqr_rot.reshape(MLA_Q_LORA, -1).astype(BF)
    wkv = w_kv_up[l].reshape(MLA_KV_LORA, MLA_HEADS, MLA_NOPE_DIM + MLA_V_DIM)
    w["w_uk_t"] = jnp.transpose(wkv[..., :MLA_NOPE_DIM], (1, 2, 0)).astype(BF)
    w["w_uv"] = jnp.transpose(wkv[..., MLA_NOPE_DIM:], (1, 0, 2)).astype(BF)
    w["w_out"] = w_out[l].astype(BF)
    w["g_mix_post"] = row(g_mix_post)
    w["g_mem_tok"] = row(g_mem_tok)
    w["w_mem_kv"] = w_mem_kv[l].astype(BF)
    w["g_mem_pre"] = row(g_mem_pre)
    w["w_mem_q"] = w_mem_q[l].astype(BF)
    w["w_mem_o"] = w_mem_o[l].astype(BF)
    w["g_mem_post"] = row(g_mem_post)
    return w


def _rope_tables(pos):
    half = MLA_ROPE_DIM // 2
    inv_freq = ROPE_THETA ** (-jnp.arange(half, dtype=F32) / half)
    ang = pos.astype(F32)[:, None] * inv_freq[None, :]
    cos, sin = jnp.cos(ang), jnp.sin(ang)
    cos_t = jnp.tile(jnp.concatenate([cos, cos], axis=1), (1, MLA_HEADS))
    sin_t = jnp.tile(jnp.concatenate([-sin, sin], axis=1), (1, MLA_HEADS))
    return cos_t, sin_t


def kernel(x_prompt, x_sample, mem_prompt, cache_fox_k, cache_fox_v, cache_fox_logf, cache_mla_ckv, cache_mla_krope, cache_mem_k, cache_mem_v, page_table, g_ffn1_pre, w_ffn1_gu, w_ffn1_down, g_ffn1_post, g_mix_pre, w_in, b_fgate, g_q_norm, w_q_up, g_kv_norm, w_kv_up, w_out, g_mix_post, g_mem_tok, w_mem_kv, g_mem_pre, w_mem_q, w_mem_o, g_mem_post, g_ffn2_pre, w_ffn2_gu, w_ffn2_down, g_ffn2_post):
    n_seq, seq_len, _ = x_prompt.shape
    n_dec, dec_seq, _ = x_sample.shape
    depth = w_in.shape[0]
    n_pages = page_table.shape[1]
    past_len = n_pages * PAGE_SIZE
    n_mem = mem_prompt.shape[1]
    assert dec_seq == 4

    xp = x_prompt.reshape(n_seq * seq_len, D_MODEL)
    xs = x_sample.reshape(n_dec * dec_seq, D_MODEL)
    mem = mem_prompt.reshape(n_seq * n_mem, D_MODEL)
    cos_p, sin_p = _rope_tables(jnp.arange(seq_len, dtype=jnp.int32))
    pos_s = past_len + jnp.arange(dec_seq, dtype=jnp.int32)
    cos_s, sin_s = _rope_tables(jnp.tile(pos_s, n_dec))

    outs = {k: [] for k in ("pk", "pv", "plf", "pckv", "pkr", "pmk", "pmv",
                            "sk", "sv", "slf", "sckv", "skr")}
    for l in range(depth):
        w = _prep_weights(l, g_ffn1_pre, w_ffn1_gu, w_ffn1_down, g_ffn1_post, g_mix_pre,
                          w_in, b_fgate, g_q_norm, w_q_up, g_kv_norm, w_kv_up, w_out,
                          g_mix_post, g_mem_tok, w_mem_kv, g_mem_pre, w_mem_q, w_mem_o,
                          g_mem_post, g_ffn2_pre, w_ffn2_gu, w_ffn2_down, g_ffn2_post)

        xp = _ffn_half(xp, *w["ffn1"])
        pp = _mix_proj(xp, seq_len, w, cos_p, sin_p)
        o_fox = _fox_prompt(pp, n_seq, seq_len)
        o_lat = _mla_prompt(pp, n_seq, seq_len)
        xp, qm = _mix_out(xp, o_fox, o_lat, w)
        mk, mv = _memory_kv(mem, w["g_mem_tok"], w["w_mem_kv"])
        om = _mem_prompt(qm, mk.reshape(n_seq, n_mem, MEM_W), mv.reshape(n_seq, n_mem, MEM_W),
                         seq_len)
        xp = _memout_ffn(xp, om, w["w_mem_o"], w["g_mem_post"], *w["ffn2"])
        outs["pk"].append(pp["k32"].reshape(n_seq, seq_len, FOX_KV_HEADS, FOX_HEAD_DIM))
        outs["pv"].append(pp["v32"].reshape(n_seq, seq_len, FOX_KV_HEADS, FOX_HEAD_DIM))
        outs["plf"].append(pp["lf"].reshape(n_seq, seq_len, FOX_HEADS))
        outs["pckv"].append(pp["ckv32"].reshape(n_seq, seq_len, MLA_KV_LORA))
        outs["pkr"].append(pp["kr32"].reshape(n_seq, seq_len, MLA_ROPE_DIM))
        outs["pmk"].append(mk.reshape(n_seq, n_mem, MEM_HEADS, MEM_HEAD_DIM))
        outs["pmv"].append(mv.reshape(n_seq, n_mem, MEM_HEADS, MEM_HEAD_DIM))

        xs = _ffn_half(xs, *w["ffn1"])
        ps = _mix_proj(xs, n_dec * dec_seq, w, cos_s, sin_s)
        n_pool = cache_fox_k.shape[1]
        kc = jnp.transpose(cache_fox_k[l], (0, 2, 3, 1)).reshape(n_pool, FOX_KV_W, PAGE_SIZE)
        vc = jnp.transpose(cache_fox_v[l], (0, 2, 3, 1)).reshape(n_pool, FOX_KV_W, PAGE_SIZE)
        krc = jnp.transpose(cache_mla_krope[l], (0, 2, 1))
        lfc = jnp.transpose(cache_fox_logf[l], (0, 2, 1))
        o_fox, o_lat = _decode_attention(ps, page_table, kc, vc, cache_mla_ckv[l], krc, lfc)
        xs, qm = _mix_out(xs, o_fox, o_lat, w)
        om = _mem_decode(qm, cache_mem_k[l], cache_mem_v[l])
        xs = _memout_ffn(xs, om, w["w_mem_o"], w["g_mem_post"], *w["ffn2"])
        outs["sk"].append(ps["k32"].reshape(n_dec, dec_seq, FOX_KV_HEADS, FOX_HEAD_DIM))
        outs["sv"].append(ps["v32"].reshape(n_dec, dec_seq, FOX_KV_HEADS, FOX_HEAD_DIM))
        outs["slf"].append(ps["lf"].reshape(n_dec, dec_seq, FOX_HEADS))
        outs["sckv"].append(ps["ckv32"].reshape(n_dec, dec_seq, MLA_KV_LORA))
        outs["skr"].append(ps["kr32"].reshape(n_dec, dec_seq, MLA_ROPE_DIM))

    st = {k: jnp.stack(v) for k, v in outs.items()}
    return (xp.reshape(n_seq, seq_len, D_MODEL), xs.reshape(n_dec, dec_seq, D_MODEL),
            st["pk"], st["pv"], st["plf"], st["pckv"], st["pkr"], st["pmk"], st["pmv"],
            st["sk"], st["sv"], st["slf"], st["sckv"], st["skr"])
```

```python
import functools

import numpy as np
import jax
import jax.numpy as jnp
from jax import lax
from jax.experimental import pallas as pl
from jax.experimental.pallas import tpu as pltpu

BF = jnp.bfloat16
F32 = jnp.float32

D_MODEL = 1024
PAGE_SIZE = 128
FOX_HEADS = 8
FOX_KV_HEADS = 4
FOX_GROUP = FOX_HEADS // FOX_KV_HEADS
FOX_HEAD_DIM = 64
FOX_Q_W = FOX_HEADS * FOX_HEAD_DIM
FOX_KV_W = FOX_KV_HEADS * FOX_HEAD_DIM
MLA_HEADS = 4
MLA_Q_LORA = 256
MLA_KV_LORA = 256
MLA_NOPE_DIM = 128
MLA_ROPE_DIM = 64
MLA_V_DIM = 128
ROPE_THETA = 10000.0
MEM_HEADS = 4
MEM_HEAD_DIM = 128
MEM_W = MEM_HEADS * MEM_HEAD_DIM
D_FF = 2816
RMS_EPS = 1e-6
NEG_INF = -1e30

LANES = 128
VMEM_LIMIT_BYTES = 56 * 1024 * 1024

LOG2E = 1.4426950408889634
FOX_SCALE = FOX_HEAD_DIM ** -0.5 * LOG2E
MLA_SCALE = (MLA_NOPE_DIM + MLA_ROPE_DIM) ** -0.5 * LOG2E
MEM_SCALE = MEM_HEAD_DIM ** -0.5

ROW_TILE = 512
FF_CHUNK = 256
ATT_TQ = 256
ATT_TK = 1024
DEC_PAGES = 32
MEM_DEC_BATCH = 8


def _params(*sem):
    return pltpu.CompilerParams(dimension_semantics=sem,
                                vmem_limit_bytes=VMEM_LIMIT_BYTES)


def _rms(x, g):
    return x * lax.rsqrt(jnp.mean(x * x, axis=-1, keepdims=True) + RMS_EPS) * g


def _dot(a, b):
    return jnp.dot(a, b, preferred_element_type=F32)


def _dot_nt(a, b):
    return lax.dot_general(a, b, (((1,), (1,)), ((), ())), preferred_element_type=F32)


def _const_spec(shape):
    zeros = (0,) * len(shape)
    return pl.BlockSpec(shape, lambda *_: zeros)


def _row_tile(n):
    return ROW_TILE if n % ROW_TILE == 0 else n


def _swiglu_half(x, gpre, wgu_ref, wd_ref, gpost):
    h = _rms(x, gpre).astype(BF)
    acc = jnp.zeros(x.shape, F32)
    for c in range(D_FF // FF_CHUNK):
        g = _dot(h, wgu_ref[:, c * FF_CHUNK:(c + 1) * FF_CHUNK])
        u = _dot(h, wgu_ref[:, D_FF + c * FF_CHUNK:D_FF + (c + 1) * FF_CHUNK])
        a = (g * jax.nn.sigmoid(g) * u).astype(BF)
        acc = acc + _dot(a, wd_ref[c * FF_CHUNK:(c + 1) * FF_CHUNK, :])
    return x + 0.5 * _rms(acc, gpost)


def _ffn_kernel(x_ref, gpre_ref, wgu_ref, wd_ref, gpost_ref, o_ref):
    o_ref[...] = _swiglu_half(x_ref[...], gpre_ref[...], wgu_ref, wd_ref, gpost_ref[...])


def _ffn_half(x, gpre, wgu, wd, gpost):
    n = x.shape[0]
    tm = _row_tile(n)
    row = pl.BlockSpec((tm, D_MODEL), lambda i: (i, 0))
    return pl.pallas_call(
        _ffn_kernel,
        grid=(n // tm,),
        in_specs=[row, _const_spec((1, D_MODEL)), _const_spec((D_MODEL, 2 * D_FF)),
                  _const_spec((D_FF, D_MODEL)), _const_spec((1, D_MODEL))],
        out_specs=row,
        out_shape=jax.ShapeDtypeStruct((n, D_MODEL), F32),
        compiler_params=_params("parallel"),
        name="ffn_half",
    )(x, gpre, wgu, wd, gpost)


def _memout_ffn_kernel(x_ref, om_ref, wmo_ref, gmpost_ref, gpre_ref, wgu_ref, wd_ref, gpost_ref,
                       o_ref):
    x = x_ref[...] + _rms(_dot(om_ref[...], wmo_ref[...]), gmpost_ref[...])
    o_ref[...] = _swiglu_half(x, gpre_ref[...], wgu_ref, wd_ref, gpost_ref[...])


def _memout_ffn(x, om, wmo, gmpost, gpre, wgu, wd, gpost):
    n = x.shape[0]
    tm = _row_tile(n)
    row = pl.BlockSpec((tm, D_MODEL), lambda i: (i, 0))
    return pl.pallas_call(
        _memout_ffn_kernel,
        grid=(n // tm,),
        in_specs=[row, pl.BlockSpec((tm, MEM_W), lambda i: (i, 0)),
                  _const_spec((MEM_W, D_MODEL)), _const_spec((1, D_MODEL)),
                  _const_spec((1, D_MODEL)), _const_spec((D_MODEL, 2 * D_FF)),
                  _const_spec((D_FF, D_MODEL)), _const_spec((1, D_MODEL))],
        out_specs=row,
        out_shape=jax.ShapeDtypeStruct((n, D_MODEL), F32),
        compiler_params=_params("parallel"),
        name="memout_ffn",
    )(x, om, wmo, gmpost, gpre, wgu, wd, gpost)


BIG_W = FOX_Q_W + 2 * FOX_KV_W + MLA_Q_LORA + MLA_KV_LORA
SMALL_W = 3 * LANES


def _log_sigmoid(x):
    return jnp.minimum(x, 0.0) - jnp.log1p(jnp.exp(-jnp.abs(x)))


MIX_COMMON = ("q", "qlat", "qrope", "ckv32")
MIX_PROMPT = ("kT32", "vT32", "krT32", "lfT", "kT", "v1", "ckv16", "ckvT", "krT", "c", "cT")
MIX_DECODE = ("k32", "v32", "kr32", "lf")


def _mixproj_kernel(x_ref, g_ref, wbig_ref, wsmall_ref, bf_ref, gq_ref, gkv_ref,
                    wqn_ref, wqra_ref, wqrb_ref, wuk_ref, cos_ref, sin_ref, *rest,
                    names, tiles_per_seq):
    o = dict(zip(names, rest))
    carry_ref = rest[-1]
    prompt = "kT" in o
    i = pl.program_id(0)
    tm = x_ref.shape[0]
    h = _rms(x_ref[...], g_ref[...]).astype(BF)
    big = _dot(h, wbig_ref[...])
    small = _dot(h, wsmall_ref[...])
    cos = cos_ref[...]
    sin = sin_ref[...]

    o["q"][...] = (big[:, :FOX_Q_W] * FOX_SCALE).astype(BF)
    k = big[:, FOX_Q_W:FOX_Q_W + FOX_KV_W]
    v = big[:, FOX_Q_W + FOX_KV_W:FOX_Q_W + 2 * FOX_KV_W]
    lf = _log_sigmoid(small[:, 2 * LANES:] + bf_ref[...])
    if prompt:
        kT = k.T
        o["kT32"][0] = kT
        o["kT"][0] = kT.astype(BF)
        o["vT32"][0] = v.T
        one_hot = (lax.broadcasted_iota(jnp.int32, (tm, LANES - FOX_HEAD_DIM), 1) == 0).astype(BF)
        for kvh in range(FOX_KV_HEADS):
            o["v1"][:, kvh * LANES:kvh * LANES + FOX_HEAD_DIM] = (
                v[:, kvh * FOX_HEAD_DIM:(kvh + 1) * FOX_HEAD_DIM].astype(BF))
            o["v1"][:, kvh * LANES + FOX_HEAD_DIM:(kvh + 1) * LANES] = one_hot

        lfT = lf.T
        o["lfT"][0] = lfT[:FOX_HEADS]

        @pl.when(i % tiles_per_seq == 0)
        def _():
            carry_ref[...] = jnp.zeros_like(carry_ref)

        lane = lax.broadcasted_iota(jnp.int32, lfT.shape, 1)
        run = lfT
        sh = 1
        while sh < tm:
            run = run + jnp.where(lane >= sh, pltpu.roll(run, sh, axis=1), 0.0)
            sh *= 2
        run = run + carry_ref[...]
        carry_ref[...] = run[:, tm - 1:tm]
        run2 = run * LOG2E
        o["cT"][0] = run2[:FOX_HEADS]
        o["c"][...] = run2.T[:, :FOX_HEADS]
    else:
        o["k32"][...] = k
        o["v32"][...] = v
        o["lf"][...] = lf[:, :FOX_HEADS]

    cq = big[:, FOX_Q_W + 2 * FOX_KV_W:FOX_Q_W + 2 * FOX_KV_W + MLA_Q_LORA]
    cqn = _rms(cq, gq_ref[...]).astype(BF)
    qn = _dot(cqn, wqn_ref[...]).astype(BF)
    for hd in range(MLA_HEADS):
        ql = _dot(qn[:, hd * MLA_NOPE_DIM:(hd + 1) * MLA_NOPE_DIM], wuk_ref[hd])
        o["qlat"][:, hd * MLA_KV_LORA:(hd + 1) * MLA_KV_LORA] = (ql * MLA_SCALE).astype(BF)
    qr = _dot(cqn, wqra_ref[...]) * cos + _dot(cqn, wqrb_ref[...]) * sin
    o["qrope"][...] = (qr * MLA_SCALE).astype(BF)

    ckv = _rms(big[:, BIG_W - MLA_KV_LORA:], gkv_ref[...])
    o["ckv32"][...] = ckv
    kr = small[:, :LANES] * cos[:, :LANES] + small[:, LANES:2 * LANES] * sin[:, :LANES]
    if prompt:
        o["ckv16"][...] = ckv.astype(BF)
        o["ckvT"][0] = ckv.T.astype(BF)
        krT = kr.T[:MLA_ROPE_DIM]
        o["krT32"][0] = krT
        o["krT"][0] = krT.astype(BF)
    else:
        o["kr32"][...] = kr[:, :MLA_ROPE_DIM]


def _mix_proj(x, seq_len, w, cos_tab, sin_tab, prompt):
    n = x.shape[0]
    tm = _row_tile(min(n, seq_len))
    n_seq = n // seq_len
    tps = seq_len // tm
    tab_tiles = cos_tab.shape[0] // tm
    row = lambda width: pl.BlockSpec((tm, width), lambda i: (i, 0))
    colT = lambda height: pl.BlockSpec((1, height, tm), lambda i: (i // tps, 0, i % tps))
    tab = pl.BlockSpec((tm, 2 * LANES), lambda i: (i % tab_tiles, 0))
    rows = lambda width, dt: (jax.ShapeDtypeStruct((n, width), dt), row(width))
    cols = lambda height, dt: (jax.ShapeDtypeStruct((n_seq, height, seq_len), dt), colT(height))
    outputs = {
        "q": rows(FOX_Q_W, BF), "qlat": rows(MLA_HEADS * MLA_KV_LORA, BF),
        "qrope": rows(MLA_HEADS * MLA_ROPE_DIM, BF), "ckv32": rows(MLA_KV_LORA, F32),
        "kT32": cols(FOX_KV_W, F32), "vT32": cols(FOX_KV_W, F32),
        "krT32": cols(MLA_ROPE_DIM, F32), "lfT": cols(FOX_HEADS, F32),
        "kT": cols(FOX_KV_W, BF), "v1": rows(FOX_KV_HEADS * LANES, BF),
        "ckv16": rows(MLA_KV_LORA, BF), "ckvT": cols(MLA_KV_LORA, BF),
        "krT": cols(MLA_ROPE_DIM, BF), "c": rows(FOX_HEADS, F32), "cT": cols(FOX_HEADS, F32),
        "k32": rows(FOX_KV_W, F32), "v32": rows(FOX_KV_W, F32),
        "kr32": rows(MLA_ROPE_DIM, F32), "lf": rows(FOX_HEADS, F32),
    }
    names = MIX_COMMON + (MIX_PROMPT if prompt else MIX_DECODE)
    out_shape = tuple(outputs[k][0] for k in names)
    out_specs = tuple(outputs[k][1] for k in names)
    in_specs = [
        row(D_MODEL), _const_spec((1, D_MODEL)), _const_spec((D_MODEL, BIG_W)),
        _const_spec((D_MODEL, SMALL_W)), _const_spec((1, LANES)),
        _const_spec((1, MLA_Q_LORA)), _const_spec((1, MLA_KV_LORA)),
        _const_spec((MLA_Q_LORA, MLA_HEADS * MLA_NOPE_DIM)),
        _const_spec((MLA_Q_LORA, MLA_HEADS * MLA_ROPE_DIM)),
        _const_spec((MLA_Q_LORA, MLA_HEADS * MLA_ROPE_DIM)),
        _const_spec((MLA_HEADS, MLA_NOPE_DIM, MLA_KV_LORA)), tab, tab,
    ]
    outs = pl.pallas_call(
        functools.partial(_mixproj_kernel, names=names, tiles_per_seq=tps),
        grid=(n // tm,),
        in_specs=in_specs,
        out_specs=out_specs,
        out_shape=out_shape,
        scratch_shapes=[pltpu.VMEM((LANES, 1), F32)],
        compiler_params=_params("arbitrary"),
        name="mix_proj",
    )(x, w["g_mix_pre"], w["w_big"], w["w_small"], w["b_f"], w["g_q_norm"],
      w["g_kv_norm"], w["w_q_nope"], w["w_q_rope_a"], w["w_q_rope_b"], w["w_uk_t"],
      cos_tab, sin_tab)
    return dict(zip(names, outs))


def _online_update(m_ref, l_ref, acc_ref, h, z, v, row_bias=None):
    m_prev = m_ref[h]
    m_z = jnp.max(z, axis=1, keepdims=True)
    if row_bias is None:
        m_new = jnp.maximum(m_prev, m_z)
        shift = m_new
    else:
        m_new = jnp.maximum(m_prev, m_z + row_bias)
        shift = m_new - row_bias
    alpha = jnp.exp2(m_prev - m_new)
    p = jnp.exp2(z - shift)
    if l_ref is not None:
        l_ref[h] = alpha * l_ref[h] + jnp.sum(p, axis=1, keepdims=True)
    acc_ref[h] = alpha * acc_ref[h] + _dot(p.astype(BF), v)
    m_ref[h] = m_new


def _init_state(m_ref, l_ref, acc_ref):
    m_ref[...] = jnp.full_like(m_ref, NEG_INF)
    if l_ref is not None:
        l_ref[...] = jnp.zeros_like(l_ref)
    acc_ref[...] = jnp.zeros_like(acc_ref)


def _causal_keep(shape, q0, k0):
    qpos = q0 + lax.broadcasted_iota(jnp.int32, shape, 0)
    kpos = k0 + lax.broadcasted_iota(jnp.int32, shape, 1)
    return qpos >= kpos


def _sweep_key_blocks(block, i, tq):
    assert ATT_TK % tq == 0
    n_full = (i * tq) // ATT_TK

    def body(kj, carry):
        block(pl.multiple_of(kj * ATT_TK, ATT_TK), None)
        return carry

    lax.fori_loop(0, n_full, body, 0)
    k0 = pl.multiple_of(n_full * ATT_TK, ATT_TK)
    block(k0, _causal_keep((tq, ATT_TK), i * tq, k0))


def _fox_prompt_kernel(q_ref, c_ref, kT_ref, v1_ref, cT_ref, o_ref, qs_ref, m_ref, acc_ref):
    i = pl.program_id(1)
    tq = q_ref.shape[0]
    for hh in range(FOX_HEADS):
        qs_ref[hh] = q_ref[:, hh * FOX_HEAD_DIM:(hh + 1) * FOX_HEAD_DIM]
    _init_state(m_ref, None, acc_ref)

    def block(k0, keep):
        keys = pl.ds(k0, ATT_TK)

        def scores(hh):
            kvh = hh // FOX_GROUP
            kT = kT_ref[0, kvh * FOX_HEAD_DIM:(kvh + 1) * FOX_HEAD_DIM, keys]
            return _dot(qs_ref[hh], kT) - cT_ref[0, hh:hh + 1, keys]

        z = scores(0)
        for hh in range(FOX_HEADS):
            z_next = scores(hh + 1) if hh + 1 < FOX_HEADS else None
            if keep is not None:
                z = jnp.where(keep, z, NEG_INF)
            kvh = hh // FOX_GROUP
            _online_update(m_ref, None, acc_ref, hh, z, v1_ref[keys, kvh * LANES:(kvh + 1) * LANES],
                           row_bias=c_ref[:, hh:hh + 1])
            z = z_next

    _sweep_key_blocks(block, i, tq)
    for hh in range(FOX_HEADS):
        acc = acc_ref[hh]
        o = acc[:, :FOX_HEAD_DIM] / acc[:, FOX_HEAD_DIM:FOX_HEAD_DIM + 1]
        o_ref[:, hh * FOX_HEAD_DIM:(hh + 1) * FOX_HEAD_DIM] = o.astype(BF)


def _fox_prompt(p, n_seq, seq_len):
    tq = ATT_TQ
    nq = seq_len // tq
    return pl.pallas_call(
        _fox_prompt_kernel,
        grid=(n_seq, nq),
        in_specs=[
            pl.BlockSpec((tq, FOX_Q_W), lambda b, i: (b * nq + i, 0)),
            pl.BlockSpec((tq, FOX_HEADS), lambda b, i: (b * nq + i, 0)),
            pl.BlockSpec((1, FOX_KV_W, seq_len), lambda b, i: (b, 0, 0)),
            pl.BlockSpec((seq_len, FOX_KV_HEADS * LANES), lambda b, i: (b, 0)),
            pl.BlockSpec((1, FOX_HEADS, seq_len), lambda b, i: (b, 0, 0)),
        ],
        out_specs=pl.BlockSpec((tq, FOX_Q_W), lambda b, i: (b * nq + i, 0)),
        out_shape=jax.ShapeDtypeStruct((n_seq * seq_len, FOX_Q_W), BF),
        scratch_shapes=[pltpu.VMEM((FOX_HEADS, tq, FOX_HEAD_DIM), BF),
                        pltpu.VMEM((FOX_HEADS, tq, 1), F32),
                        pltpu.VMEM((FOX_HEADS, tq, LANES), F32)],
        compiler_params=_params("parallel", "arbitrary"),
        name="fox_prompt",
    )(p["q"], p["c"], p["kT"], p["v1"], p["cT"])


def _mla_prompt_kernel(ql_ref, qr_ref, ckvT_ref, krT_ref, ckv_ref, o_ref, qrs_ref, m_ref, l_ref,
                       acc_ref):
    i = pl.program_id(1)
    tq = ql_ref.shape[0]
    for hd in range(MLA_HEADS):
        qrs_ref[hd] = qr_ref[:, hd * MLA_ROPE_DIM:(hd + 1) * MLA_ROPE_DIM]
    _init_state(m_ref, l_ref, acc_ref)

    def block(k0, keep):
        keys = pl.ds(k0, ATT_TK)
        ckvT = ckvT_ref[0, :, keys]
        krT = krT_ref[0, :, keys]
        ckv = ckv_ref[keys, :]

        def scores(hd):
            return (_dot(ql_ref[:, hd * MLA_KV_LORA:(hd + 1) * MLA_KV_LORA], ckvT)
                    + _dot(qrs_ref[hd], krT))

        z = scores(0)
        for hd in range(MLA_HEADS):
            z_next = scores(hd + 1) if hd + 1 < MLA_HEADS else None
            if keep is not None:
                z = jnp.where(keep, z, NEG_INF)
            _online_update(m_ref, l_ref, acc_ref, hd, z, ckv)
            z = z_next

    _sweep_key_blocks(block, i, tq)
    for hd in range(MLA_HEADS):
        o_ref[:, hd * MLA_KV_LORA:(hd + 1) * MLA_KV_LORA] = (acc_ref[hd] / l_ref[hd]).astype(BF)


def _mla_prompt(p, n_seq, seq_len):
    tq = ATT_TQ
    nq = seq_len // tq
    lat_w = MLA_HEADS * MLA_KV_LORA
    return pl.pallas_call(
        _mla_prompt_kernel,
        grid=(n_seq, nq),
        in_specs=[
            pl.BlockSpec((tq, lat_w), lambda b, i: (b * nq + i, 0)),
            pl.BlockSpec((tq, MLA_HEADS * MLA_ROPE_DIM), lambda b, i: (b * nq + i, 0)),
            pl.BlockSpec((1, MLA_KV_LORA, seq_len), lambda b, i: (b, 0, 0)),
            pl.BlockSpec((1, MLA_ROPE_DIM, seq_len), lambda b, i: (b, 0, 0)),
            pl.BlockSpec((seq_len, MLA_KV_LORA), lambda b, i: (b, 0)),
        ],
        out_specs=pl.BlockSpec((tq, lat_w), lambda b, i: (b * nq + i, 0)),
        out_shape=jax.ShapeDtypeStruct((n_seq * seq_len, lat_w), BF),
        scratch_shapes=[pltpu.VMEM((MLA_HEADS, tq, MLA_ROPE_DIM), BF),
                        pltpu.VMEM((MLA_HEADS, tq, 1), F32),
                        pltpu.VMEM((MLA_HEADS, tq, 1), F32),
                        pltpu.VMEM((MLA_HEADS, tq, MLA_KV_LORA), F32)],
        compiler_params=_params("parallel", "arbitrary"),
        name="mla_prompt",
    )(p["qlat"], p["qrope"], p["ckvT"], p["krT"], p["ckv16"])


DEC_FOX_ROWS = 4 * FOX_HEADS
DEC_MLA_ROWS = 4 * MLA_HEADS


def _decode_kernel(pt_ref, qbd_ref, lfnew_ref, knew_ref, vnew_ref, qlat_ref, qrope_ref,
                   ckvnew_ref, krnew_ref, kc_ref, vc_ref, ckvc_ref, krc_ref, lfc_ref,
                   of_ref, om_ref,
                   kbuf, vbuf, ckvbuf, krbuf, lfbuf, sems,
                   mf_ref, lf_ref, accf_ref, mm_ref, lm_ref, accm_ref, rcarry_ref,
                   *, n_chunks, n_steps):
    t = pl.program_id(0)
    n_keys = DEC_PAGES * PAGE_SIZE

    def chunk_copies(step, slot):
        b = step // n_chunks
        first = (n_chunks - 1 - step % n_chunks) * DEC_PAGES
        copies = []
        for j in range(DEC_PAGES):
            pid = pt_ref[b, first + j]
            lanes = pl.ds(j * PAGE_SIZE, PAGE_SIZE)
            copies += [
                pltpu.make_async_copy(kc_ref.at[pid], kbuf.at[slot, :, lanes], sems.at[slot, 0]),
                pltpu.make_async_copy(vc_ref.at[pid], vbuf.at[slot, :, lanes], sems.at[slot, 1]),
                pltpu.make_async_copy(ckvc_ref.at[pid], ckvbuf.at[slot, lanes, :], sems.at[slot, 2]),
                pltpu.make_async_copy(krc_ref.at[pid], krbuf.at[slot, :, lanes], sems.at[slot, 3]),
                pltpu.make_async_copy(lfc_ref.at[pid], lfbuf.at[slot, :, lanes], sems.at[slot, 4]),
            ]
        return copies

    slot = t % 2
    t_next = jnp.where(t + 1 < n_steps, t + 1, 0)

    @pl.when(t == 0)
    def _():
        for cp in chunk_copies(t, slot):
            cp.start()

    @pl.when(t % n_chunks == 0)
    def _():
        mf_ref[...] = jnp.full_like(mf_ref, NEG_INF)
        lf_ref[...] = jnp.zeros_like(lf_ref)
        accf_ref[...] = jnp.zeros_like(accf_ref)
        mm_ref[...] = jnp.full_like(mm_ref, NEG_INF)
        lm_ref[...] = jnp.zeros_like(lm_ref)
        accm_ref[...] = jnp.zeros_like(accm_ref)
        rcarry_ref[...] = jnp.zeros_like(rcarry_ref)

    for cp in chunk_copies(t, slot):
        cp.wait()

    lfnew = lfnew_ref[0]
    parts = [lfnew[0:FOX_HEADS]]
    for u in range(1, 4):
        parts.append(parts[-1] + lfnew[u * FOX_HEADS:(u + 1) * FOX_HEADS])
    ncol = jnp.concatenate(parts, axis=0) * LOG2E
    qbd = qbd_ref[0]

    lfp = lfbuf[slot]
    lane = lax.broadcasted_iota(jnp.int32, lfp.shape, 1)
    run = lfp
    sh = 1
    while sh < n_keys:
        run = run + jnp.where(lane < n_keys - sh, pltpu.roll(run, n_keys - sh, axis=1), 0.0)
        sh *= 2
    later = run - lfp + rcarry_ref[...]
    rcarry_ref[...] = rcarry_ref[...] + run[:, 0:1]
    bias = jnp.concatenate([later * LOG2E] * 4, axis=0) + ncol

    def update(m_ref, l_ref, acc_ref, s, pv_fn):
        m_prev = m_ref[...]
        m_new = jnp.maximum(m_prev, jnp.max(s, axis=1, keepdims=True))
        alpha = jnp.exp2(m_prev - m_new)
        p = jnp.exp2(s - m_new)
        l_ref[...] = alpha * l_ref[...] + jnp.sum(p, axis=1, keepdims=True)
        acc_ref[...] = alpha * acc_ref[...] + pv_fn(p)
        m_ref[...] = m_new

    s_f = _dot(qbd, kbuf[slot].astype(BF)) + bias
    ckv = ckvbuf[slot].astype(BF)
    qlat = qlat_ref[0]
    qrope = qrope_ref[0]
    s_m = _dot_nt(qlat, ckv) + _dot(qrope, krbuf[slot].astype(BF))

    for cp in chunk_copies(t_next, 1 - slot):
        cp.start()

    update(mf_ref, lf_ref, accf_ref, s_f,
           lambda p: _dot_nt(p.astype(BF), vbuf[slot].astype(BF)))
    update(mm_ref, lm_ref, accm_ref, s_m, lambda p: _dot(p.astype(BF), ckv))

    @pl.when(t % n_chunks == n_chunks - 1)
    def _():
        qf = qbd.astype(F32)
        rowf = lax.broadcasted_iota(jnp.int32, (DEC_FOX_ROWS, 1), 0)
        knew = knew_ref[0]
        vnew = vnew_ref[0]
        for u in range(4):
            s = jnp.sum(qf * knew[u:u + 1, :], axis=1, keepdims=True)
            n_u = jnp.concatenate([ncol[u * FOX_HEADS:(u + 1) * FOX_HEADS]] * 4, axis=0)
            s = jnp.where(rowf >= u * FOX_HEADS, s + (ncol - n_u), NEG_INF)
            update(mf_ref, lf_ref, accf_ref, s, lambda p: p * vnew[u:u + 1, :])
        o = accf_ref[...] / lf_ref[...]
        row = lax.broadcasted_iota(jnp.int32, o.shape, 0)
        col = lax.broadcasted_iota(jnp.int32, o.shape, 1)
        o = jnp.where(col // FOX_HEAD_DIM == (row % FOX_HEADS) // FOX_GROUP, o, 0.0)
        of_ref[0] = (o[:, 0:64] + o[:, 64:128]) + (o[:, 128:192] + o[:, 192:256])

        qlf = qlat.astype(F32)
        qrf = qrope.astype(F32)
        rowm = lax.broadcasted_iota(jnp.int32, (DEC_MLA_ROWS, 1), 0)
        ckvnew = ckvnew_ref[0]
        krnew = krnew_ref[0]
        for u in range(4):
            s = (jnp.sum(qlf * ckvnew[u:u + 1, :], axis=1, keepdims=True)
                 + jnp.sum(qrf * krnew[u:u + 1, :], axis=1, keepdims=True))
            s = jnp.where(rowm >= u * MLA_HEADS, s, NEG_INF)
            update(mm_ref, lm_ref, accm_ref, s, lambda p: p * ckvnew[u:u + 1, :])
        om_ref[0] = accm_ref[...] / lm_ref[...]

    @pl.when(t == n_steps - 1)
    def _():
        for cp in chunk_copies(t_next, 1 - slot):
            cp.wait()


def _decode_attention(ps, page_table, kc, vc, ckvc, krc, lfc):
    n_b, n_pages = page_table.shape
    n_chunks = n_pages // DEC_PAGES
    n_steps = n_b * n_chunks
    n_keys = DEC_PAGES * PAGE_SIZE

    q = ps["q"].reshape(n_b, 4, FOX_KV_HEADS, FOX_GROUP, FOX_HEAD_DIM)
    eye = jnp.eye(FOX_KV_HEADS, dtype=BF)
    qbd = jnp.einsum("btkgd,kj->btkgjd", q, eye).reshape(n_b, DEC_FOX_ROWS, FOX_KV_W)
    lfnew = ps["lf"].reshape(n_b, DEC_FOX_ROWS, 1)
    knew = ps["k32"].reshape(n_b, 4, FOX_KV_W)
    vnew = ps["v32"].reshape(n_b, 4, FOX_KV_W)
    qlat = ps["qlat"].reshape(n_b, DEC_MLA_ROWS, MLA_KV_LORA)
    qrope = ps["qrope"].reshape(n_b, DEC_MLA_ROWS, MLA_ROPE_DIM)
    ckvnew = ps["ckv32"].reshape(n_b, 4, MLA_KV_LORA)
    krnew = ps["kr32"].reshape(n_b, 4, MLA_ROPE_DIM)

    per_b = lambda r, w: pl.BlockSpec((1, r, w), lambda t, pt: (t // n_chunks, 0, 0))
    hbm = pl.BlockSpec(memory_space=pl.ANY)
    grid_spec = pltpu.PrefetchScalarGridSpec(
        num_scalar_prefetch=1,
        grid=(n_steps,),
        in_specs=[per_b(DEC_FOX_ROWS, FOX_KV_W), per_b(DEC_FOX_ROWS, 1),
                  per_b(4, FOX_KV_W), per_b(4, FOX_KV_W),
                  per_b(DEC_MLA_ROWS, MLA_KV_LORA), per_b(DEC_MLA_ROWS, MLA_ROPE_DIM),
                  per_b(4, MLA_KV_LORA), per_b(4, MLA_ROPE_DIM),
                  hbm, hbm, hbm, hbm, hbm],
        out_specs=[per_b(DEC_FOX_ROWS, FOX_HEAD_DIM), per_b(DEC_MLA_ROWS, MLA_KV_LORA)],
        scratch_shapes=[
            pltpu.VMEM((2, FOX_KV_W, n_keys), F32),
            pltpu.VMEM((2, FOX_KV_W, n_keys), F32),
            pltpu.VMEM((2, n_keys, MLA_KV_LORA), F32),
            pltpu.VMEM((2, MLA_ROPE_DIM, n_keys), F32),
            pltpu.VMEM((2, FOX_HEADS, n_keys), F32),
            pltpu.SemaphoreType.DMA((2, 5)),
            pltpu.VMEM((DEC_FOX_ROWS, 1), F32), pltpu.VMEM((DEC_FOX_ROWS, 1), F32),
            pltpu.VMEM((DEC_FOX_ROWS, FOX_KV_W), F32),
            pltpu.VMEM((DEC_MLA_ROWS, 1), F32), pltpu.VMEM((DEC_MLA_ROWS, 1), F32),
            pltpu.VMEM((DEC_MLA_ROWS, MLA_KV_LORA), F32),
            pltpu.VMEM((FOX_HEADS, 1), F32),
        ],
    )
    o_fox, o_lat = pl.pallas_call(
        functools.partial(_decode_kernel, n_chunks=n_chunks, n_steps=n_steps),
        grid_spec=grid_spec,
        out_shape=(jax.ShapeDtypeStruct((n_b, DEC_FOX_ROWS, FOX_HEAD_DIM), F32),
                   jax.ShapeDtypeStruct((n_b, DEC_MLA_ROWS, MLA_KV_LORA), F32)),
        compiler_params=_params("arbitrary"),
        name="decode_attention",
    )(page_table, qbd, lfnew, knew, vnew, qlat, qrope, ckvnew, krnew, kc, vc, ckvc, krc, lfc)
    return (o_fox.reshape(n_b * 4, FOX_Q_W).astype(BF),
            o_lat.reshape(n_b * 4, MLA_HEADS * MLA_KV_LORA).astype(BF))


def _mixout_kernel(x_ref, of_ref, ol_ref, wuv_ref, wo_ref, gpost_ref, gmpre_ref, wmq_ref,
                   x_out_ref, qm_ref):
    parts = [of_ref[...]]
    for hd in range(MLA_HEADS):
        om = _dot(ol_ref[:, hd * MLA_KV_LORA:(hd + 1) * MLA_KV_LORA], wuv_ref[hd])
        parts.append(om.astype(BF))
    o = jnp.concatenate(parts, axis=1)
    x = x_ref[...] + _rms(_dot(o, wo_ref[...]), gpost_ref[...])
    x_out_ref[...] = x
    hm = _rms(x, gmpre_ref[...]).astype(BF)
    qm_ref[...] = (_dot(hm, wmq_ref[...]) * MEM_SCALE).astype(BF)


def _mix_out(x, o_fox, o_lat, w):
    n = x.shape[0]
    tm = _row_tile(n)
    row = lambda width: pl.BlockSpec((tm, width), lambda i: (i, 0))
    mix_w = FOX_Q_W + MLA_HEADS * MLA_V_DIM
    return pl.pallas_call(
        _mixout_kernel,
        grid=(n // tm,),
        in_specs=[row(D_MODEL), row(FOX_Q_W), row(MLA_HEADS * MLA_KV_LORA),
                  _const_spec((MLA_HEADS, MLA_KV_LORA, MLA_V_DIM)),
                  _const_spec((mix_w, D_MODEL)), _const_spec((1, D_MODEL)),
                  _const_spec((1, D_MODEL)), _const_spec((D_MODEL, MEM_W))],
        out_specs=(row(D_MODEL), row(MEM_W)),
        out_shape=(jax.ShapeDtypeStruct((n, D_MODEL), F32),
                   jax.ShapeDtypeStruct((n, MEM_W), BF)),
        compiler_params=_params("parallel"),
        name="mix_out",
    )(x, o_fox, o_lat, w["w_uv"], w["w_out"], w["g_mix_post"], w["g_mem_pre"], w["w_mem_q"])


def _memkv_kernel(mem_ref, g_ref, w_ref, k_ref, v_ref):
    kv = _dot(_rms(mem_ref[...], g_ref[...]).astype(BF), w_ref[...])
    k_ref[...] = kv[:, :MEM_W]
    v_ref[...] = kv[:, MEM_W:]


def _memory_kv(mem, g, w):
    n = mem.shape[0]
    return pl.pallas_call(
        _memkv_kernel,
        grid=(1,),
        in_specs=[_const_spec((n, D_MODEL)), _const_spec((1, D_MODEL)),
                  _const_spec((D_MODEL, 2 * MEM_W))],
        out_specs=(_const_spec((n, MEM_W)), _const_spec((n, MEM_W))),
        out_shape=(jax.ShapeDtypeStruct((n, MEM_W), F32),) * 2,
        compiler_params=_params("arbitrary"),
        name="memory_kv",
    )(mem, g, w)


def _softmax_rows(s):
    p = jnp.exp(s - jnp.max(s, axis=1, keepdims=True))
    return p, jnp.sum(p, axis=1, keepdims=True)


def _mem_prompt_kernel(q_ref, k_ref, v_ref, o_ref):
    for hd in range(MEM_HEADS):
        sl = slice(hd * MEM_HEAD_DIM, (hd + 1) * MEM_HEAD_DIM)
        p, l = _softmax_rows(_dot_nt(q_ref[:, sl], k_ref[0, :, sl].astype(BF)))
        o = _dot(p.astype(BF), v_ref[0, :, sl].astype(BF))
        o_ref[:, sl] = (o / l).astype(BF)


def _mem_prompt(qm, mk, mv, seq_len):
    n = qm.shape[0]
    tm = _row_tile(seq_len)
    tps = seq_len // tm
    n_mem = mk.shape[1]
    kv = pl.BlockSpec((1, n_mem, MEM_W), lambda i: (i // tps, 0, 0))
    return pl.pallas_call(
        _mem_prompt_kernel,
        grid=(n // tm,),
        in_specs=[pl.BlockSpec((tm, MEM_W), lambda i: (i, 0)), kv, kv],
        out_specs=pl.BlockSpec((tm, MEM_W), lambda i: (i, 0)),
        out_shape=jax.ShapeDtypeStruct((n, MEM_W), BF),
        compiler_params=_params("parallel"),
        name="mem_prompt",
    )(qm, mk, mv)


def _mem_decode_kernel(q_ref, k_ref, v_ref, o_ref):
    rows = q_ref.shape[1]
    row = lax.broadcasted_iota(jnp.int32, (rows, MEM_W), 0)
    col = lax.broadcasted_iota(jnp.int32, (rows, MEM_W), 1)
    own = col // MEM_HEAD_DIM == row % MEM_HEADS
    for b in range(q_ref.shape[0]):
        k = jnp.concatenate([k_ref[b, :, hd, :] for hd in range(MEM_HEADS)], axis=1).astype(BF)
        v = jnp.concatenate([v_ref[b, :, hd, :] for hd in range(MEM_HEADS)], axis=1).astype(BF)
        p, l = _softmax_rows(_dot_nt(q_ref[b], k))
        o = jnp.where(own, _dot(p.astype(BF), v) / l, 0.0)
        o_ref[b] = ((o[:, 0:128] + o[:, 128:256]) + (o[:, 256:384] + o[:, 384:512])).astype(BF)


def _mem_decode(qm, cache_k, cache_v):
    n_b = cache_k.shape[0]
    n_mem = cache_k.shape[1]
    rows = 4 * MEM_HEADS
    q = qm.reshape(n_b, 4, MEM_HEADS, MEM_HEAD_DIM)
    eye = jnp.eye(MEM_HEADS, dtype=BF)
    qbd = jnp.einsum("bthd,hj->bthjd", q, eye).reshape(n_b, rows, MEM_W)
    g = MEM_DEC_BATCH if n_b % MEM_DEC_BATCH == 0 else n_b
    kv = pl.BlockSpec((g, n_mem, MEM_HEADS, MEM_HEAD_DIM), lambda i: (i, 0, 0, 0))
    o = pl.pallas_call(
        _mem_decode_kernel,
        grid=(n_b // g,),
        in_specs=[pl.BlockSpec((g, rows, MEM_W), lambda i: (i, 0, 0)), kv, kv],
        out_specs=pl.BlockSpec((g, rows, MEM_HEAD_DIM), lambda i: (i, 0, 0)),
        out_shape=jax.ShapeDtypeStruct((n_b, rows, MEM_HEAD_DIM), BF),
        compiler_params=_params("parallel"),
        name="mem_decode",
    )(qbd, cache_k, cache_v)
    return o.reshape(n_b * 4, MEM_W)


def _prep_weights(l, g_ffn1_pre, w_ffn1_gu, w_ffn1_down, g_ffn1_post, g_mix_pre, w_in,
                  b_fgate, g_q_norm, w_q_up, g_kv_norm, w_kv_up, w_out, g_mix_post,
                  g_mem_tok, w_mem_kv, g_mem_pre, w_mem_q, w_mem_o, g_mem_post,
                  g_ffn2_pre, w_ffn2_gu, w_ffn2_down, g_ffn2_post):
    row = lambda g: g[l].reshape(1, -1)
    w = {}
    for name, (gpre, wgu, wd, gpost) in {
            "ffn1": (g_ffn1_pre, w_ffn1_gu, w_ffn1_down, g_ffn1_post),
            "ffn2": (g_ffn2_pre, w_ffn2_gu, w_ffn2_down, g_ffn2_post)}.items():
        w[name] = (row(gpre), wgu[l].astype(BF), wd[l].astype(BF), row(gpost))
    cuts = np.cumsum([FOX_Q_W, FOX_KV_W, FOX_KV_W, FOX_HEADS, MLA_Q_LORA, MLA_KV_LORA]).tolist()
    wq, wk, wv, wf, wcq, wckv, wkr = jnp.split(w_in[l], cuts, axis=1)
    half = MLA_ROPE_DIM // 2
    pad = lambda a: jnp.pad(a, ((0, 0), (0, LANES - a.shape[1])))
    wkr_rot = jnp.concatenate([wkr[:, half:], wkr[:, :half]], axis=1)
    w["g_mix_pre"] = row(g_mix_pre)
    w["w_big"] = jnp.concatenate([wq, wk, wv, wcq, wckv], axis=1).astype(BF)
    w["w_small"] = jnp.concatenate([pad(wkr), pad(wkr_rot), pad(wf)], axis=1).astype(BF)
    w["b_f"] = pad(b_fgate[l].reshape(1, -1))
    w["g_q_norm"] = row(g_q_norm)
    w["g_kv_norm"] = row(g_kv_norm)
    wqu = w_q_up[l].reshape(MLA_Q_LORA, MLA_HEADS, MLA_NOPE_DIM + MLA_ROPE_DIM)
    w["w_q_nope"] = wqu[..., :MLA_NOPE_DIM].reshape(MLA_Q_LORA, -1).astype(BF)
    wqr = wqu[..., MLA_NOPE_DIM:]
    wqr_rot = jnp.concatenate([wqr[..., half:], wqr[..., :half]], axis=-1)
    w["w_q_rope_a"] = wqr.reshape(MLA_Q_LORA, -1).astype(BF)
    w["w_q_rope_b"] = wqr_rot.reshape(MLA_Q_LORA, -1).astype(BF)
    wkv = w_kv_up[l].reshape(MLA_KV_LORA, MLA_HEADS, MLA_NOPE_DIM + MLA_V_DIM)
    w["w_uk_t"] = jnp.transpose(wkv[..., :MLA_NOPE_DIM], (1, 2, 0)).astype(BF)
    w["w_uv"] = jnp.transpose(wkv[..., MLA_NOPE_DIM:], (1, 0, 2)).astype(BF)
    w["w_out"] = w_out[l].astype(BF)
    w["g_mix_post"] = row(g_mix_post)
    w["g_mem_tok"] = row(g_mem_tok)
    w["w_mem_kv"] = w_mem_kv[l].astype(BF)
    w["g_mem_pre"] = row(g_mem_pre)
    w["w_mem_q"] = w_mem_q[l].astype(BF)
    w["w_mem_o"] = w_mem_o[l].astype(BF)
    w["g_mem_post"] = row(g_mem_post)
    return w


def _rope_tables(pos):
    half = MLA_ROPE_DIM // 2
    inv_freq = ROPE_THETA ** (-jnp.arange(half, dtype=F32) / half)
    ang = pos.astype(F32)[:, None] * inv_freq[None, :]
    cos, sin = jnp.cos(ang), jnp.sin(ang)
    cos_t = jnp.tile(jnp.concatenate([cos, cos], axis=1), (1, MLA_HEADS))
    sin_t = jnp.tile(jnp.concatenate([-sin, sin], axis=1), (1, MLA_HEADS))
    return cos_t, sin_t


def kernel(x_prompt, x_sample, mem_prompt, cache_fox_k, cache_fox_v, cache_fox_logf, cache_mla_ckv, cache_mla_krope, cache_mem_k, cache_mem_v, page_table, g_ffn1_pre, w_ffn1_gu, w_ffn1_down, g_ffn1_post, g_mix_pre, w_in, b_fgate, g_q_norm, w_q_up, g_kv_norm, w_kv_up, w_out, g_mix_post, g_mem_tok, w_mem_kv, g_mem_pre, w_mem_q, w_mem_o, g_mem_post, g_ffn2_pre, w_ffn2_gu, w_ffn2_down, g_ffn2_post):
    n_seq, seq_len, _ = x_prompt.shape
    n_dec, dec_seq, _ = x_sample.shape
    depth = w_in.shape[0]
    n_pages = page_table.shape[1]
    past_len = n_pages * PAGE_SIZE
    n_mem = mem_prompt.shape[1]
    assert dec_seq == 4

    xp = x_prompt.reshape(n_seq * seq_len, D_MODEL)
    xs = x_sample.reshape(n_dec * dec_seq, D_MODEL)
    mem = mem_prompt.reshape(n_seq * n_mem, D_MODEL)
    cos_p, sin_p = _rope_tables(jnp.arange(seq_len, dtype=jnp.int32))
    pos_s = past_len + jnp.arange(dec_seq, dtype=jnp.int32)
    cos_s, sin_s = _rope_tables(jnp.tile(pos_s, n_dec))

    outs = {k: [] for k in ("pk", "pv", "plf", "pckv", "pkr", "pmk", "pmv",
                            "sk", "sv", "slf", "sckv", "skr")}
    for l in range(depth):
        w = _prep_weights(l, g_ffn1_pre, w_ffn1_gu, w_ffn1_down, g_ffn1_post, g_mix_pre,
                          w_in, b_fgate, g_q_norm, w_q_up, g_kv_norm, w_kv_up, w_out,
                          g_mix_post, g_mem_tok, w_mem_kv, g_mem_pre, w_mem_q, w_mem_o,
                          g_mem_post, g_ffn2_pre, w_ffn2_gu, w_ffn2_down, g_ffn2_post)

        xp = _ffn_half(xp, *w["ffn1"])
        pp = _mix_proj(xp, seq_len, w, cos_p, sin_p, prompt=True)
        o_fox = _fox_prompt(pp, n_seq, seq_len)
        o_lat = _mla_prompt(pp, n_seq, seq_len)
        xp, qm = _mix_out(xp, o_fox, o_lat, w)
        mk, mv = _memory_kv(mem, w["g_mem_tok"], w["w_mem_kv"])
        om = _mem_prompt(qm, mk.reshape(n_seq, n_mem, MEM_W), mv.reshape(n_seq, n_mem, MEM_W),
                         seq_len)
        xp = _memout_ffn(xp, om, w["w_mem_o"], w["g_mem_post"], *w["ffn2"])
        heads_last = lambda a: jnp.transpose(
            a.reshape(n_seq, FOX_KV_HEADS, FOX_HEAD_DIM, seq_len), (0, 3, 1, 2))
        outs["pk"].append(heads_last(pp["kT32"]))
        outs["pv"].append(heads_last(pp["vT32"]))
        outs["plf"].append(jnp.transpose(pp["lfT"], (0, 2, 1)))
        outs["pckv"].append(pp["ckv32"].reshape(n_seq, seq_len, MLA_KV_LORA))
        outs["pkr"].append(jnp.transpose(pp["krT32"], (0, 2, 1)))
        outs["pmk"].append(mk.reshape(n_seq, n_mem, MEM_HEADS, MEM_HEAD_DIM))
        outs["pmv"].append(mv.reshape(n_seq, n_mem, MEM_HEADS, MEM_HEAD_DIM))

        xs = _ffn_half(xs, *w["ffn1"])
        ps = _mix_proj(xs, n_dec * dec_seq, w, cos_s, sin_s, prompt=False)
        n_pool = cache_fox_k.shape[1]
        kc = jnp.transpose(cache_fox_k[l], (0, 2, 3, 1)).reshape(n_pool, FOX_KV_W, PAGE_SIZE)
        vc = jnp.transpose(cache_fox_v[l], (0, 2, 3, 1)).reshape(n_pool, FOX_KV_W, PAGE_SIZE)
        krc = jnp.transpose(cache_mla_krope[l], (0, 2, 1))
        lfc = jnp.transpose(cache_fox_logf[l], (0, 2, 1))
        o_fox, o_lat = _decode_attention(ps, page_table, kc, vc, cache_mla_ckv[l], krc, lfc)
        xs, qm = _mix_out(xs, o_fox, o_lat, w)
        om = _mem_decode(qm, cache_mem_k[l], cache_mem_v[l])
        xs = _memout_ffn(xs, om, w["w_mem_o"], w["g_mem_post"], *w["ffn2"])
        outs["sk"].append(ps["k32"].reshape(n_dec, dec_seq, FOX_KV_HEADS, FOX_HEAD_DIM))
        outs["sv"].append(ps["v32"].reshape(n_dec, dec_seq, FOX_KV_HEADS, FOX_HEAD_DIM))
        outs["slf"].append(ps["lf"].reshape(n_dec, dec_seq, FOX_HEADS))
        outs["sckv"].append(ps["ckv32"].reshape(n_dec, dec_seq, MLA_KV_LORA))
        outs["skr"].append(ps["kr32"].reshape(n_dec, dec_seq, MLA_ROPE_DIM))

    st = {k: jnp.stack(v) for k, v in outs.items()}
    return (xp.reshape(n_seq, seq_len, D_MODEL), xs.reshape(n_dec, dec_seq, D_MODEL),
            st["pk"], st["pv"], st["plf"], st["pckv"], st["pkr"], st["pmk"], st["pmv"],
            st["sk"], st["sv"], st["slf"], st["sckv"], st["skr"])
```

```python
import functools

import numpy as np
import jax
import jax.numpy as jnp
from jax import lax
from jax.experimental import pallas as pl
from jax.experimental.pallas import tpu as pltpu

BF = jnp.bfloat16
F32 = jnp.float32

D_MODEL = 1024
PAGE_SIZE = 128
FOX_HEADS = 8
FOX_KV_HEADS = 4
FOX_GROUP = FOX_HEADS // FOX_KV_HEADS
FOX_HEAD_DIM = 64
FOX_Q_W = FOX_HEADS * FOX_HEAD_DIM
FOX_KV_W = FOX_KV_HEADS * FOX_HEAD_DIM
MLA_HEADS = 4
MLA_Q_LORA = 256
MLA_KV_LORA = 256
MLA_NOPE_DIM = 128
MLA_ROPE_DIM = 64
MLA_V_DIM = 128
ROPE_THETA = 10000.0
MEM_HEADS = 4
MEM_HEAD_DIM = 128
MEM_W = MEM_HEADS * MEM_HEAD_DIM
D_FF = 2816
RMS_EPS = 1e-6
NEG_INF = -1e30

LANES = 128
VMEM_LIMIT_BYTES = 56 * 1024 * 1024

LOG2E = 1.4426950408889634
FOX_SCALE = FOX_HEAD_DIM ** -0.5 * LOG2E
MLA_SCALE = (MLA_NOPE_DIM + MLA_ROPE_DIM) ** -0.5 * LOG2E
MEM_SCALE = MEM_HEAD_DIM ** -0.5

ROW_TILE = 512
FF_CHUNK = 256
ATT_TQ = 256
ATT_TK = 1024
DEC_PAGES = 32
MEM_DEC_BATCH = 8


def _params(*sem):
    return pltpu.CompilerParams(dimension_semantics=sem,
                                vmem_limit_bytes=VMEM_LIMIT_BYTES)


def _rms(x, g):
    return x * lax.rsqrt(jnp.mean(x * x, axis=-1, keepdims=True) + RMS_EPS) * g


def _dot(a, b):
    return jnp.dot(a, b, preferred_element_type=F32)


def _dot_nt(a, b):
    return lax.dot_general(a, b, (((1,), (1,)), ((), ())), preferred_element_type=F32)


def _const_spec(shape):
    zeros = (0,) * len(shape)
    return pl.BlockSpec(shape, lambda *_: zeros)


def _row_tile(n):
    return ROW_TILE if n % ROW_TILE == 0 else n


def _swiglu_half(x, gpre, wgu_ref, wd_ref, gpost):
    h = _rms(x, gpre).astype(BF)
    acc = jnp.zeros(x.shape, F32)
    for c in range(D_FF // FF_CHUNK):
        g = _dot(h, wgu_ref[:, c * FF_CHUNK:(c + 1) * FF_CHUNK])
        u = _dot(h, wgu_ref[:, D_FF + c * FF_CHUNK:D_FF + (c + 1) * FF_CHUNK])
        a = (g * jax.nn.sigmoid(g) * u).astype(BF)
        acc = acc + _dot(a, wd_ref[c * FF_CHUNK:(c + 1) * FF_CHUNK, :])
    return x + 0.5 * _rms(acc, gpost)


def _ffn_kernel(x_ref, gpre_ref, wgu_ref, wd_ref, gpost_ref, o_ref):
    o_ref[...] = _swiglu_half(x_ref[...], gpre_ref[...], wgu_ref, wd_ref, gpost_ref[...])


def _ffn_half(x, gpre, wgu, wd, gpost):
    n = x.shape[0]
    tm = _row_tile(n)
    row = pl.BlockSpec((tm, D_MODEL), lambda i: (i, 0))
    return pl.pallas_call(
        _ffn_kernel,
        grid=(n // tm,),
        in_specs=[row, _const_spec((1, D_MODEL)), _const_spec((D_MODEL, 2 * D_FF)),
                  _const_spec((D_FF, D_MODEL)), _const_spec((1, D_MODEL))],
        out_specs=row,
        out_shape=jax.ShapeDtypeStruct((n, D_MODEL), F32),
        compiler_params=_params("parallel"),
        name="ffn_half",
    )(x, gpre, wgu, wd, gpost)


def _memout_ffn_kernel(x_ref, om_ref, wmo_ref, gmpost_ref, gpre_ref, wgu_ref, wd_ref, gpost_ref,
                       o_ref):
    x = x_ref[...] + _rms(_dot(om_ref[...], wmo_ref[...]), gmpost_ref[...])
    o_ref[...] = _swiglu_half(x, gpre_ref[...], wgu_ref, wd_ref, gpost_ref[...])


def _memout_ffn(x, om, wmo, gmpost, gpre, wgu, wd, gpost):
    n = x.shape[0]
    tm = _row_tile(n)
    row = pl.BlockSpec((tm, D_MODEL), lambda i: (i, 0))
    return pl.pallas_call(
        _memout_ffn_kernel,
        grid=(n // tm,),
        in_specs=[row, pl.BlockSpec((tm, MEM_W), lambda i: (i, 0)),
                  _const_spec((MEM_W, D_MODEL)), _const_spec((1, D_MODEL)),
                  _const_spec((1, D_MODEL)), _const_spec((D_MODEL, 2 * D_FF)),
                  _const_spec((D_FF, D_MODEL)), _const_spec((1, D_MODEL))],
        out_specs=row,
        out_shape=jax.ShapeDtypeStruct((n, D_MODEL), F32),
        compiler_params=_params("parallel"),
        name="memout_ffn",
    )(x, om, wmo, gmpost, gpre, wgu, wd, gpost)


BIG_W = FOX_Q_W + 2 * FOX_KV_W + MLA_Q_LORA + MLA_KV_LORA
SMALL_W = 3 * LANES


def _log_sigmoid(x):
    return jnp.minimum(x, 0.0) - jnp.log1p(jnp.exp(-jnp.abs(x)))


MIX_COMMON = ("q", "qlat", "qrope", "ckv32")
MIX_PROMPT = ("kT32", "vT32", "krT32", "lfT", "kT", "v1", "ckv16", "ckvT", "krT", "c", "cT")
MIX_DECODE = ("k32", "v32", "kr32", "lf")


def _mixproj_kernel(x_ref, g_ref, wbig_ref, wsmall_ref, bf_ref, gq_ref, gkv_ref,
                    wqn_ref, wqra_ref, wqrb_ref, wuk_ref, cos_ref, sin_ref, *rest,
                    names, tiles_per_seq):
    o = dict(zip(names, rest))
    carry_ref = rest[-1]
    prompt = "kT" in o
    i = pl.program_id(0)
    tm = x_ref.shape[0]
    h = _rms(x_ref[...], g_ref[...]).astype(BF)
    big = _dot(h, wbig_ref[...])
    small = _dot(h, wsmall_ref[...])
    cos = cos_ref[...]
    sin = sin_ref[...]

    o["q"][...] = (big[:, :FOX_Q_W] * FOX_SCALE).astype(BF)
    k = big[:, FOX_Q_W:FOX_Q_W + FOX_KV_W]
    v = big[:, FOX_Q_W + FOX_KV_W:FOX_Q_W + 2 * FOX_KV_W]
    lf = _log_sigmoid(small[:, 2 * LANES:] + bf_ref[...])
    if prompt:
        kT = k.T
        o["kT32"][0] = kT
        o["kT"][0] = kT.astype(BF)
        o["vT32"][0] = v.T
        one_hot = (lax.broadcasted_iota(jnp.int32, (tm, LANES - FOX_HEAD_DIM), 1) == 0).astype(BF)
        for kvh in range(FOX_KV_HEADS):
            o["v1"][:, kvh * LANES:kvh * LANES + FOX_HEAD_DIM] = (
                v[:, kvh * FOX_HEAD_DIM:(kvh + 1) * FOX_HEAD_DIM].astype(BF))
            o["v1"][:, kvh * LANES + FOX_HEAD_DIM:(kvh + 1) * LANES] = one_hot

        lfT = lf.T
        o["lfT"][0] = lfT[:FOX_HEADS]

        @pl.when(i % tiles_per_seq == 0)
        def _():
            carry_ref[...] = jnp.zeros_like(carry_ref)

        lane = lax.broadcasted_iota(jnp.int32, lfT.shape, 1)
        run = lfT
        sh = 1
        while sh < tm:
            run = run + jnp.where(lane >= sh, pltpu.roll(run, sh, axis=1), 0.0)
            sh *= 2
        run = run + carry_ref[...]
        carry_ref[...] = run[:, tm - 1:tm]
        run2 = run * LOG2E
        o["cT"][0] = run2[:FOX_HEADS]
        o["c"][...] = run2.T[:, :FOX_HEADS]
    else:
        o["k32"][...] = k
        o["v32"][...] = v
        o["lf"][...] = lf[:, :FOX_HEADS]

    cq = big[:, FOX_Q_W + 2 * FOX_KV_W:FOX_Q_W + 2 * FOX_KV_W + MLA_Q_LORA]
    cqn = _rms(cq, gq_ref[...]).astype(BF)
    qn = _dot(cqn, wqn_ref[...]).astype(BF)
    for hd in range(MLA_HEADS):
        ql = _dot(qn[:, hd * MLA_NOPE_DIM:(hd + 1) * MLA_NOPE_DIM], wuk_ref[hd])
        o["qlat"][:, hd * MLA_KV_LORA:(hd + 1) * MLA_KV_LORA] = (ql * MLA_SCALE).astype(BF)
    qr = _dot(cqn, wqra_ref[...]) * cos + _dot(cqn, wqrb_ref[...]) * sin
    o["qrope"][...] = (qr * MLA_SCALE).astype(BF)

    ckv = _rms(big[:, BIG_W - MLA_KV_LORA:], gkv_ref[...])
    o["ckv32"][...] = ckv
    kr = small[:, :LANES] * cos[:, :LANES] + small[:, LANES:2 * LANES] * sin[:, :LANES]
    if prompt:
        o["ckv16"][...] = ckv.astype(BF)
        o["ckvT"][0] = ckv.T.astype(BF)
        krT = kr.T[:MLA_ROPE_DIM]
        o["krT32"][0] = krT
        o["krT"][0] = krT.astype(BF)
    else:
        o["kr32"][...] = kr[:, :MLA_ROPE_DIM]


def _mix_proj(x, seq_len, w, cos_tab, sin_tab, prompt):
    n = x.shape[0]
    tm = _row_tile(min(n, seq_len))
    n_seq = n // seq_len
    tps = seq_len // tm
    tab_tiles = cos_tab.shape[0] // tm
    row = lambda width: pl.BlockSpec((tm, width), lambda i: (i, 0))
    colT = lambda height: pl.BlockSpec((1, height, tm), lambda i: (i // tps, 0, i % tps))
    tab = pl.BlockSpec((tm, 2 * LANES), lambda i: (i % tab_tiles, 0))
    rows = lambda width, dt: (jax.ShapeDtypeStruct((n, width), dt), row(width))
    cols = lambda height, dt: (jax.ShapeDtypeStruct((n_seq, height, seq_len), dt), colT(height))
    outputs = {
        "q": rows(FOX_Q_W, BF), "qlat": rows(MLA_HEADS * MLA_KV_LORA, BF),
        "qrope": rows(MLA_HEADS * MLA_ROPE_DIM, BF), "ckv32": rows(MLA_KV_LORA, F32),
        "kT32": cols(FOX_KV_W, F32), "vT32": cols(FOX_KV_W, F32),
        "krT32": cols(MLA_ROPE_DIM, F32), "lfT": cols(FOX_HEADS, F32),
        "kT": cols(FOX_KV_W, BF), "v1": rows(FOX_KV_HEADS * LANES, BF),
        "ckv16": rows(MLA_KV_LORA, BF), "ckvT": cols(MLA_KV_LORA, BF),
        "krT": cols(MLA_ROPE_DIM, BF), "c": rows(FOX_HEADS, F32), "cT": cols(FOX_HEADS, F32),
        "k32": rows(FOX_KV_W, F32), "v32": rows(FOX_KV_W, F32),
        "kr32": rows(MLA_ROPE_DIM, F32), "lf": rows(FOX_HEADS, F32),
    }
    names = MIX_COMMON + (MIX_PROMPT if prompt else MIX_DECODE)
    out_shape = tuple(outputs[k][0] for k in names)
    out_specs = tuple(outputs[k][1] for k in names)
    in_specs = [
        row(D_MODEL), _const_spec((1, D_MODEL)), _const_spec((D_MODEL, BIG_W)),
        _const_spec((D_MODEL, SMALL_W)), _const_spec((1, LANES)),
        _const_spec((1, MLA_Q_LORA)), _const_spec((1, MLA_KV_LORA)),
        _const_spec((MLA_Q_LORA, MLA_HEADS * MLA_NOPE_DIM)),
        _const_spec((MLA_Q_LORA, MLA_HEADS * MLA_ROPE_DIM)),
        _const_spec((MLA_Q_LORA, MLA_HEADS * MLA_ROPE_DIM)),
        _const_spec((MLA_HEADS, MLA_NOPE_DIM, MLA_KV_LORA)), tab, tab,
    ]
    outs = pl.pallas_call(
        functools.partial(_mixproj_kernel, names=names, tiles_per_seq=tps),
        grid=(n // tm,),
        in_specs=in_specs,
        out_specs=out_specs,
        out_shape=out_shape,
        scratch_shapes=[pltpu.VMEM((LANES, 1), F32)],
        compiler_params=_params("arbitrary"),
        name="mix_proj",
    )(x, w["g_mix_pre"], w["w_big"], w["w_small"], w["b_f"], w["g_q_norm"],
      w["g_kv_norm"], w["w_q_nope"], w["w_q_rope_a"], w["w_q_rope_b"], w["w_uk_t"],
      cos_tab, sin_tab)
    return dict(zip(names, outs))


def _online_update(m_ref, l_ref, acc_ref, h, z, v, row_bias=None):
    m_prev = m_ref[h]
    m_z = jnp.max(z, axis=1, keepdims=True)
    if row_bias is None:
        m_new = jnp.maximum(m_prev, m_z)
        shift = m_new
    else:
        m_new = jnp.maximum(m_prev, m_z + row_bias)
        shift = m_new - row_bias
    alpha = jnp.exp2(m_prev - m_new)
    p = jnp.exp2(z - shift)
    if l_ref is not None:
        l_ref[h] = alpha * l_ref[h] + jnp.sum(p, axis=1, keepdims=True)
    acc_ref[h] = alpha * acc_ref[h] + _dot(p.astype(BF), v)
    m_ref[h] = m_new


def _init_state(m_ref, l_ref, acc_ref):
    m_ref[...] = jnp.full_like(m_ref, NEG_INF)
    if l_ref is not None:
        l_ref[...] = jnp.zeros_like(l_ref)
    acc_ref[...] = jnp.zeros_like(acc_ref)


def _causal_keep(shape, q0, k0):
    qpos = q0 + lax.broadcasted_iota(jnp.int32, shape, 0)
    kpos = k0 + lax.broadcasted_iota(jnp.int32, shape, 1)
    return qpos >= kpos


def _sweep_key_blocks(block, i, tq):
    assert ATT_TK % tq == 0
    n_full = (i * tq) // ATT_TK

    def body(kj, carry):
        block(pl.multiple_of(kj * ATT_TK, ATT_TK), None)
        return carry

    lax.fori_loop(0, n_full, body, 0)
    k0 = pl.multiple_of(n_full * ATT_TK, ATT_TK)
    block(k0, _causal_keep((tq, ATT_TK), i * tq, k0))


def _fox_prompt_kernel(q_ref, c_ref, kT_ref, v1_ref, cT_ref, o_ref, qs_ref, m_ref, acc_ref):
    i = pl.program_id(1)
    tq = q_ref.shape[0]
    for hh in range(FOX_HEADS):
        qs_ref[hh] = q_ref[:, hh * FOX_HEAD_DIM:(hh + 1) * FOX_HEAD_DIM]
    _init_state(m_ref, None, acc_ref)

    def block(k0, keep):
        keys = pl.ds(k0, ATT_TK)

        def scores(hh):
            kvh = hh // FOX_GROUP
            kT = kT_ref[0, kvh * FOX_HEAD_DIM:(kvh + 1) * FOX_HEAD_DIM, keys]
            return _dot(qs_ref[hh], kT) - cT_ref[0, hh:hh + 1, keys]

        z = scores(0)
        for hh in range(FOX_HEADS):
            z_next = scores(hh + 1) if hh + 1 < FOX_HEADS else None
            if keep is not None:
                z = jnp.where(keep, z, NEG_INF)
            kvh = hh // FOX_GROUP
            _online_update(m_ref, None, acc_ref, hh, z, v1_ref[keys, kvh * LANES:(kvh + 1) * LANES],
                           row_bias=c_ref[:, hh:hh + 1])
            z = z_next

    _sweep_key_blocks(block, i, tq)
    for hh in range(FOX_HEADS):
        acc = acc_ref[hh]
        o = acc[:, :FOX_HEAD_DIM] / acc[:, FOX_HEAD_DIM:FOX_HEAD_DIM + 1]
        o_ref[:, hh * FOX_HEAD_DIM:(hh + 1) * FOX_HEAD_DIM] = o.astype(BF)


def _fox_prompt(p, n_seq, seq_len):
    tq = ATT_TQ
    nq = seq_len // tq
    return pl.pallas_call(
        _fox_prompt_kernel,
        grid=(n_seq, nq),
        in_specs=[
            pl.BlockSpec((tq, FOX_Q_W), lambda b, i: (b * nq + i, 0)),
            pl.BlockSpec((tq, FOX_HEADS), lambda b, i: (b * nq + i, 0)),
            pl.BlockSpec((1, FOX_KV_W, seq_len), lambda b, i: (b, 0, 0)),
            pl.BlockSpec((seq_len, FOX_KV_HEADS * LANES), lambda b, i: (b, 0)),
            pl.BlockSpec((1, FOX_HEADS, seq_len), lambda b, i: (b, 0, 0)),
        ],
        out_specs=pl.BlockSpec((tq, FOX_Q_W), lambda b, i: (b * nq + i, 0)),
        out_shape=jax.ShapeDtypeStruct((n_seq * seq_len, FOX_Q_W), BF),
        scratch_shapes=[pltpu.VMEM((FOX_HEADS, tq, FOX_HEAD_DIM), BF),
                        pltpu.VMEM((FOX_HEADS, tq, 1), F32),
                        pltpu.VMEM((FOX_HEADS, tq, LANES), F32)],
        compiler_params=_params("parallel", "arbitrary"),
        name="fox_prompt",
    )(p["q"], p["c"], p["kT"], p["v1"], p["cT"])


def _mla_prompt_kernel(ql_ref, qr_ref, ckvT_ref, krT_ref, ckv_ref, o_ref, qrs_ref, m_ref, l_ref,
                       acc_ref):
    i = pl.program_id(1)
    tq = ql_ref.shape[0]
    for hd in range(MLA_HEADS):
        qrs_ref[hd] = qr_ref[:, hd * MLA_ROPE_DIM:(hd + 1) * MLA_ROPE_DIM]
    _init_state(m_ref, l_ref, acc_ref)

    def block(k0, keep):
        keys = pl.ds(k0, ATT_TK)
        ckvT = ckvT_ref[0, :, keys]
        krT = krT_ref[0, :, keys]
        ckv = ckv_ref[keys, :]

        def scores(hd):
            return (_dot(ql_ref[:, hd * MLA_KV_LORA:(hd + 1) * MLA_KV_LORA], ckvT)
                    + _dot(qrs_ref[hd], krT))

        z = scores(0)
        for hd in range(MLA_HEADS):
            z_next = scores(hd + 1) if hd + 1 < MLA_HEADS else None
            if keep is not None:
                z = jnp.where(keep, z, NEG_INF)
            _online_update(m_ref, l_ref, acc_ref, hd, z, ckv)
            z = z_next

    _sweep_key_blocks(block, i, tq)
    for hd in range(MLA_HEADS):
        o_ref[:, hd * MLA_KV_LORA:(hd + 1) * MLA_KV_LORA] = (acc_ref[hd] / l_ref[hd]).astype(BF)


def _mla_prompt(p, n_seq, seq_len):
    tq = ATT_TQ
    nq = seq_len // tq
    lat_w = MLA_HEADS * MLA_KV_LORA
    return pl.pallas_call(
        _mla_prompt_kernel,
        grid=(n_seq, nq),
        in_specs=[
            pl.BlockSpec((tq, lat_w), lambda b, i: (b * nq + i, 0)),
            pl.BlockSpec((tq, MLA_HEADS * MLA_ROPE_DIM), lambda b, i: (b * nq + i, 0)),
            pl.BlockSpec((1, MLA_KV_LORA, seq_len), lambda b, i: (b, 0, 0)),
            pl.BlockSpec((1, MLA_ROPE_DIM, seq_len), lambda b, i: (b, 0, 0)),
            pl.BlockSpec((seq_len, MLA_KV_LORA), lambda b, i: (b, 0)),
        ],
        out_specs=pl.BlockSpec((tq, lat_w), lambda b, i: (b * nq + i, 0)),
        out_shape=jax.ShapeDtypeStruct((n_seq * seq_len, lat_w), BF),
        scratch_shapes=[pltpu.VMEM((MLA_HEADS, tq, MLA_ROPE_DIM), BF),
                        pltpu.VMEM((MLA_HEADS, tq, 1), F32),
                        pltpu.VMEM((MLA_HEADS, tq, 1), F32),
                        pltpu.VMEM((MLA_HEADS, tq, MLA_KV_LORA), F32)],
        compiler_params=_params("parallel", "arbitrary"),
        name="mla_prompt",
    )(p["qlat"], p["qrope"], p["ckvT"], p["krT"], p["ckv16"])


DEC_FOX_ROWS = 4 * FOX_HEADS
DEC_MLA_ROWS = 4 * MLA_HEADS


def _decode_kernel(pt_ref, qbd_ref, lfnew_ref, knew_ref, vnew_ref, qlat_ref, qrope_ref,
                   ckvnew_ref, krnew_ref, kc_ref, vc_ref, ckvc_ref, krc_ref, lfc_ref,
                   of_ref, om_ref,
                   kbuf, vbuf, ckvbuf, krbuf, lfbuf, sems,
                   mf_ref, lf_ref, accf_ref, mm_ref, lm_ref, accm_ref, rcarry_ref,
                   *, n_chunks, n_steps):
    t = pl.program_id(0)
    n_keys = DEC_PAGES * PAGE_SIZE

    def chunk_copies(step, slot):
        b = step // n_chunks
        first = (n_chunks - 1 - step % n_chunks) * DEC_PAGES
        copies = []
        for j in range(DEC_PAGES):
            pid = pt_ref[b, first + j]
            lanes = pl.ds(j * PAGE_SIZE, PAGE_SIZE)
            copies += [
                pltpu.make_async_copy(kc_ref.at[pid], kbuf.at[slot, :, lanes], sems.at[slot, 0]),
                pltpu.make_async_copy(vc_ref.at[pid], vbuf.at[slot, :, lanes], sems.at[slot, 1]),
                pltpu.make_async_copy(ckvc_ref.at[pid], ckvbuf.at[slot, lanes, :], sems.at[slot, 2]),
                pltpu.make_async_copy(krc_ref.at[pid], krbuf.at[slot, :, lanes], sems.at[slot, 3]),
                pltpu.make_async_copy(lfc_ref.at[pid], lfbuf.at[slot, :, lanes], sems.at[slot, 4]),
            ]
        return copies

    slot = t % 2

    @pl.when(t == 0)
    def _():
        for cp in chunk_copies(t, slot):
            cp.start()

    @pl.when(t + 1 < n_steps)
    def _():
        for cp in chunk_copies(t + 1, 1 - slot):
            cp.start()

    @pl.when(t % n_chunks == 0)
    def _():
        mf_ref[...] = jnp.full_like(mf_ref, NEG_INF)
        lf_ref[...] = jnp.zeros_like(lf_ref)
        accf_ref[...] = jnp.zeros_like(accf_ref)
        mm_ref[...] = jnp.full_like(mm_ref, NEG_INF)
        lm_ref[...] = jnp.zeros_like(lm_ref)
        accm_ref[...] = jnp.zeros_like(accm_ref)
        rcarry_ref[...] = jnp.zeros_like(rcarry_ref)

    for cp in chunk_copies(t, slot):
        cp.wait()

    lfnew = lfnew_ref[0]
    parts = [lfnew[0:FOX_HEADS]]
    for u in range(1, 4):
        parts.append(parts[-1] + lfnew[u * FOX_HEADS:(u + 1) * FOX_HEADS])
    ncol = jnp.concatenate(parts, axis=0) * LOG2E
    qbd = qbd_ref[0]

    lfp = lfbuf[slot]
    lane = lax.broadcasted_iota(jnp.int32, lfp.shape, 1)
    run = lfp
    sh = 1
    while sh < n_keys:
        run = run + jnp.where(lane < n_keys - sh, pltpu.roll(run, n_keys - sh, axis=1), 0.0)
        sh *= 2
    later = run - lfp + rcarry_ref[...]
    rcarry_ref[...] = rcarry_ref[...] + run[:, 0:1]
    bias = jnp.concatenate([later * LOG2E] * 4, axis=0) + ncol

    def update(m_ref, l_ref, acc_ref, s, pv_fn):
        m_prev = m_ref[...]
        m_new = jnp.maximum(m_prev, jnp.max(s, axis=1, keepdims=True))
        alpha = jnp.exp2(m_prev - m_new)
        p = jnp.exp2(s - m_new)
        l_ref[...] = alpha * l_ref[...] + jnp.sum(p, axis=1, keepdims=True)
        acc_ref[...] = alpha * acc_ref[...] + pv_fn(p)
        m_ref[...] = m_new

    s_f = _dot(qbd, kbuf[slot].astype(BF)) + bias
    ckv = ckvbuf[slot].astype(BF)
    qlat = qlat_ref[0]
    qrope = qrope_ref[0]
    s_m = _dot_nt(qlat, ckv) + _dot(qrope, krbuf[slot].astype(BF))
    update(mf_ref, lf_ref, accf_ref, s_f,
           lambda p: _dot_nt(p.astype(BF), vbuf[slot].astype(BF)))
    update(mm_ref, lm_ref, accm_ref, s_m, lambda p: _dot(p.astype(BF), ckv))

    @pl.when(t % n_chunks == n_chunks - 1)
    def _():
        qf = qbd.astype(F32)
        rowf = lax.broadcasted_iota(jnp.int32, (DEC_FOX_ROWS, 1), 0)
        knew = knew_ref[0]
        vnew = vnew_ref[0]
        for u in range(4):
            s = jnp.sum(qf * knew[u:u + 1, :], axis=1, keepdims=True)
            n_u = jnp.concatenate([ncol[u * FOX_HEADS:(u + 1) * FOX_HEADS]] * 4, axis=0)
            s = jnp.where(rowf >= u * FOX_HEADS, s + (ncol - n_u), NEG_INF)
            update(mf_ref, lf_ref, accf_ref, s, lambda p: p * vnew[u:u + 1, :])
        o = accf_ref[...] / lf_ref[...]
        row = lax.broadcasted_iota(jnp.int32, o.shape, 0)
        col = lax.broadcasted_iota(jnp.int32, o.shape, 1)
        o = jnp.where(col // FOX_HEAD_DIM == (row % FOX_HEADS) // FOX_GROUP, o, 0.0)
        of_ref[0] = (o[:, 0:64] + o[:, 64:128]) + (o[:, 128:192] + o[:, 192:256])

        qlf = qlat.astype(F32)
        qrf = qrope.astype(F32)
        rowm = lax.broadcasted_iota(jnp.int32, (DEC_MLA_ROWS, 1), 0)
        ckvnew = ckvnew_ref[0]
        krnew = krnew_ref[0]
        for u in range(4):
            s = (jnp.sum(qlf * ckvnew[u:u + 1, :], axis=1, keepdims=True)
                 + jnp.sum(qrf * krnew[u:u + 1, :], axis=1, keepdims=True))
            s = jnp.where(rowm >= u * MLA_HEADS, s, NEG_INF)
            update(mm_ref, lm_ref, accm_ref, s, lambda p: p * ckvnew[u:u + 1, :])
        om_ref[0] = accm_ref[...] / lm_ref[...]


def _decode_attention(ps, page_table, kc, vc, ckvc, krc, lfc):
    n_b, n_pages = page_table.shape
    n_chunks = n_pages // DEC_PAGES
    n_steps = n_b * n_chunks
    n_keys = DEC_PAGES * PAGE_SIZE

    q = ps["q"].reshape(n_b, 4, FOX_KV_HEADS, FOX_GROUP, FOX_HEAD_DIM)
    eye = jnp.eye(FOX_KV_HEADS, dtype=BF)
    qbd = jnp.einsum("btkgd,kj->btkgjd", q, eye).reshape(n_b, DEC_FOX_ROWS, FOX_KV_W)
    lfnew = ps["lf"].reshape(n_b, DEC_FOX_ROWS, 1)
    knew = ps["k32"].reshape(n_b, 4, FOX_KV_W)
    vnew = ps["v32"].reshape(n_b, 4, FOX_KV_W)
    qlat = ps["qlat"].reshape(n_b, DEC_MLA_ROWS, MLA_KV_LORA)
    qrope = ps["qrope"].reshape(n_b, DEC_MLA_ROWS, MLA_ROPE_DIM)
    ckvnew = ps["ckv32"].reshape(n_b, 4, MLA_KV_LORA)
    krnew = ps["kr32"].reshape(n_b, 4, MLA_ROPE_DIM)

    per_b = lambda r, w: pl.BlockSpec((1, r, w), lambda t, pt: (t // n_chunks, 0, 0))
    hbm = pl.BlockSpec(memory_space=pl.ANY)
    grid_spec = pltpu.PrefetchScalarGridSpec(
        num_scalar_prefetch=1,
        grid=(n_steps,),
        in_specs=[per_b(DEC_FOX_ROWS, FOX_KV_W), per_b(DEC_FOX_ROWS, 1),
                  per_b(4, FOX_KV_W), per_b(4, FOX_KV_W),
                  per_b(DEC_MLA_ROWS, MLA_KV_LORA), per_b(DEC_MLA_ROWS, MLA_ROPE_DIM),
                  per_b(4, MLA_KV_LORA), per_b(4, MLA_ROPE_DIM),
                  hbm, hbm, hbm, hbm, hbm],
        out_specs=[per_b(DEC_FOX_ROWS, FOX_HEAD_DIM), per_b(DEC_MLA_ROWS, MLA_KV_LORA)],
        scratch_shapes=[
            pltpu.VMEM((2, FOX_KV_W, n_keys), F32),
            pltpu.VMEM((2, FOX_KV_W, n_keys), F32),
            pltpu.VMEM((2, n_keys, MLA_KV_LORA), F32),
            pltpu.VMEM((2, MLA_ROPE_DIM, n_keys), F32),
            pltpu.VMEM((2, FOX_HEADS, n_keys), F32),
            pltpu.SemaphoreType.DMA((2, 5)),
            pltpu.VMEM((DEC_FOX_ROWS, 1), F32), pltpu.VMEM((DEC_FOX_ROWS, 1), F32),
            pltpu.VMEM((DEC_FOX_ROWS, FOX_KV_W), F32),
            pltpu.VMEM((DEC_MLA_ROWS, 1), F32), pltpu.VMEM((DEC_MLA_ROWS, 1), F32),
            pltpu.VMEM((DEC_MLA_ROWS, MLA_KV_LORA), F32),
            pltpu.VMEM((FOX_HEADS, 1), F32),
        ],
    )
    o_fox, o_lat = pl.pallas_call(
        functools.partial(_decode_kernel, n_chunks=n_chunks, n_steps=n_steps),
        grid_spec=grid_spec,
        out_shape=(jax.ShapeDtypeStruct((n_b, DEC_FOX_ROWS, FOX_HEAD_DIM), F32),
                   jax.ShapeDtypeStruct((n_b, DEC_MLA_ROWS, MLA_KV_LORA), F32)),
        compiler_params=_params("arbitrary"),
        name="decode_attention",
    )(page_table, qbd, lfnew, knew, vnew, qlat, qrope, ckvnew, krnew, kc, vc, ckvc, krc, lfc)
    return (o_fox.reshape(n_b * 4, FOX_Q_W).astype(BF),
            o_lat.reshape(n_b * 4, MLA_HEADS * MLA_KV_LORA).astype(BF))


def _mixout_kernel(x_ref, of_ref, ol_ref, wuv_ref, wo_ref, gpost_ref, gmpre_ref, wmq_ref,
                   x_out_ref, qm_ref):
    parts = [of_ref[...]]
    for hd in range(MLA_HEADS):
        om = _dot(ol_ref[:, hd * MLA_KV_LORA:(hd + 1) * MLA_KV_LORA], wuv_ref[hd])
        parts.append(om.astype(BF))
    o = jnp.concatenate(parts, axis=1)
    x = x_ref[...] + _rms(_dot(o, wo_ref[...]), gpost_ref[...])
    x_out_ref[...] = x
    hm = _rms(x, gmpre_ref[...]).astype(BF)
    qm_ref[...] = (_dot(hm, wmq_ref[...]) * MEM_SCALE).astype(BF)


def _mix_out(x, o_fox, o_lat, w):
    n = x.shape[0]
    tm = _row_tile(n)
    row = lambda width: pl.BlockSpec((tm, width), lambda i: (i, 0))
    mix_w = FOX_Q_W + MLA_HEADS * MLA_V_DIM
    return pl.pallas_call(
        _mixout_kernel,
        grid=(n // tm,),
        in_specs=[row(D_MODEL), row(FOX_Q_W), row(MLA_HEADS * MLA_KV_LORA),
                  _const_spec((MLA_HEADS, MLA_KV_LORA, MLA_V_DIM)),
                  _const_spec((mix_w, D_MODEL)), _const_spec((1, D_MODEL)),
                  _const_spec((1, D_MODEL)), _const_spec((D_MODEL, MEM_W))],
        out_specs=(row(D_MODEL), row(MEM_W)),
        out_shape=(jax.ShapeDtypeStruct((n, D_MODEL), F32),
                   jax.ShapeDtypeStruct((n, MEM_W), BF)),
        compiler_params=_params("parallel"),
        name="mix_out",
    )(x, o_fox, o_lat, w["w_uv"], w["w_out"], w["g_mix_post"], w["g_mem_pre"], w["w_mem_q"])


def _memkv_kernel(mem_ref, g_ref, w_ref, k_ref, v_ref):
    kv = _dot(_rms(mem_ref[...], g_ref[...]).astype(BF), w_ref[...])
    k_ref[...] = kv[:, :MEM_W]
    v_ref[...] = kv[:, MEM_W:]


def _memory_kv(mem, g, w):
    n = mem.shape[0]
    return pl.pallas_call(
        _memkv_kernel,
        grid=(1,),
        in_specs=[_const_spec((n, D_MODEL)), _const_spec((1, D_MODEL)),
                  _const_spec((D_MODEL, 2 * MEM_W))],
        out_specs=(_const_spec((n, MEM_W)), _const_spec((n, MEM_W))),
        out_shape=(jax.ShapeDtypeStruct((n, MEM_W), F32),) * 2,
        compiler_params=_params("arbitrary"),
        name="memory_kv",
    )(mem, g, w)


def _softmax_rows(s):
    p = jnp.exp(s - jnp.max(s, axis=1, keepdims=True))
    return p, jnp.sum(p, axis=1, keepdims=True)


def _mem_prompt_kernel(q_ref, k_ref, v_ref, o_ref):
    for hd in range(MEM_HEADS):
        sl = slice(hd * MEM_HEAD_DIM, (hd + 1) * MEM_HEAD_DIM)
        p, l = _softmax_rows(_dot_nt(q_ref[:, sl], k_ref[0, :, sl].astype(BF)))
        o = _dot(p.astype(BF), v_ref[0, :, sl].astype(BF))
        o_ref[:, sl] = (o / l).astype(BF)


def _mem_prompt(qm, mk, mv, seq_len):
    n = qm.shape[0]
    tm = _row_tile(seq_len)
    tps = seq_len // tm
    n_mem = mk.shape[1]
    kv = pl.BlockSpec((1, n_mem, MEM_W), lambda i: (i // tps, 0, 0))
    return pl.pallas_call(
        _mem_prompt_kernel,
        grid=(n // tm,),
        in_specs=[pl.BlockSpec((tm, MEM_W), lambda i: (i, 0)), kv, kv],
        out_specs=pl.BlockSpec((tm, MEM_W), lambda i: (i, 0)),
        out_shape=jax.ShapeDtypeStruct((n, MEM_W), BF),
        compiler_params=_params("parallel"),
        name="mem_prompt",
    )(qm, mk, mv)


def _mem_decode_kernel(q_ref, k_ref, v_ref, o_ref):
    rows, n_kv = q_ref.shape[1], k_ref.shape[1]
    r_head = lax.broadcasted_iota(jnp.int32, (rows, n_kv), 0) & (MEM_HEADS - 1)
    c_head = lax.broadcasted_iota(jnp.int32, (rows, n_kv), 1) & (MEM_HEADS - 1)
    own = r_head == c_head
    n_b = q_ref.shape[0]
    scores = [_dot_nt(q_ref[b], k_ref[b].astype(BF)) for b in range(n_b)]
    for b in range(n_b):
        p, l = _softmax_rows(jnp.where(own, scores[b], NEG_INF))
        o_ref[b] = (_dot(p.astype(BF), v_ref[b].astype(BF)) / l).astype(BF)


def _mem_decode(qm, cache_k, cache_v):
    n_b, n_mem = cache_k.shape[:2]
    assert MEM_HEADS & (MEM_HEADS - 1) == 0
    rows = 4 * MEM_HEADS
    n_kv = n_mem * MEM_HEADS
    g = MEM_DEC_BATCH if n_b % MEM_DEC_BATCH == 0 else n_b
    kv = pl.BlockSpec((g, n_kv, MEM_HEAD_DIM), lambda i: (i, 0, 0))
    qo = pl.BlockSpec((g, rows, MEM_HEAD_DIM), lambda i: (i, 0, 0))
    o = pl.pallas_call(
        _mem_decode_kernel,
        grid=(n_b // g,),
        in_specs=[qo, kv, kv],
        out_specs=qo,
        out_shape=jax.ShapeDtypeStruct((n_b, rows, MEM_HEAD_DIM), BF),
        compiler_params=_params("parallel"),
        name="mem_decode",
    )(qm.reshape(n_b, rows, MEM_HEAD_DIM), cache_k.reshape(n_b, n_kv, MEM_HEAD_DIM),
      cache_v.reshape(n_b, n_kv, MEM_HEAD_DIM))
    return o.reshape(n_b * 4, MEM_W)


def _prep_weights(l, g_ffn1_pre, w_ffn1_gu, w_ffn1_down, g_ffn1_post, g_mix_pre, w_in,
                  b_fgate, g_q_norm, w_q_up, g_kv_norm, w_kv_up, w_out, g_mix_post,
                  g_mem_tok, w_mem_kv, g_mem_pre, w_mem_q, w_mem_o, g_mem_post,
                  g_ffn2_pre, w_ffn2_gu, w_ffn2_down, g_ffn2_post):
    row = lambda g: g[l].reshape(1, -1)
    w = {}
    for name, (gpre, wgu, wd, gpost) in {
            "ffn1": (g_ffn1_pre, w_ffn1_gu, w_ffn1_down, g_ffn1_post),
            "ffn2": (g_ffn2_pre, w_ffn2_gu, w_ffn2_down, g_ffn2_post)}.items():
        w[name] = (row(gpre), wgu[l].astype(BF), wd[l].astype(BF), row(gpost))
    cuts = np.cumsum([FOX_Q_W, FOX_KV_W, FOX_KV_W, FOX_HEADS, MLA_Q_LORA, MLA_KV_LORA]).tolist()
    wq, wk, wv, wf, wcq, wckv, wkr = jnp.split(w_in[l], cuts, axis=1)
    half = MLA_ROPE_DIM // 2
    pad = lambda a: jnp.pad(a, ((0, 0), (0, LANES - a.shape[1])))
    wkr_rot = jnp.concatenate([wkr[:, half:], wkr[:, :half]], axis=1)
    w["g_mix_pre"] = row(g_mix_pre)
    w["w_big"] = jnp.concatenate([wq, wk, wv, wcq, wckv], axis=1).astype(BF)
    w["w_small"] = jnp.concatenate([pad(wkr), pad(wkr_rot), pad(wf)], axis=1).astype(BF)
    w["b_f"] = pad(b_fgate[l].reshape(1, -1))
    w["g_q_norm"] = row(g_q_norm)
    w["g_kv_norm"] = row(g_kv_norm)
    wqu = w_q_up[l].reshape(MLA_Q_LORA, MLA_HEADS, MLA_NOPE_DIM + MLA_ROPE_DIM)
    w["w_q_nope"] = wqu[..., :MLA_NOPE_DIM].reshape(MLA_Q_LORA, -1).astype(BF)
    wqr = wqu[..., MLA_NOPE_DIM:]
    wqr_rot = jnp.concatenate([wqr[..., half:], wqr[..., :half]], axis=-1)
    w["w_q_rope_a"] = wqr.reshape(MLA_Q_LORA, -1).astype(BF)
    w["w_q_rope_b"] = wqr_rot.reshape(MLA_Q_LORA, -1).astype(BF)
    wkv = w_kv_up[l].reshape(MLA_KV_LORA, MLA_HEADS, MLA_NOPE_DIM + MLA_V_DIM)
    w["w_uk_t"] = jnp.transpose(wkv[..., :MLA_NOPE_DIM], (1, 2, 0)).astype(BF)
    w["w_uv"] = jnp.transpose(wkv[..., MLA_NOPE_DIM:], (1, 0, 2)).astype(BF)
    w["w_out"] = w_out[l].astype(BF)
    w["g_mix_post"] = row(g_mix_post)
    w["g_mem_tok"] = row(g_mem_tok)
    w["w_mem_kv"] = w_mem_kv[l].astype(BF)
    w["g_mem_pre"] = row(g_mem_pre)
    w["w_mem_q"] = w_mem_q[l].astype(BF)
    w["w_mem_o"] = w_mem_o[l].astype(BF)
    w["g_mem_post"] = row(g_mem_post)
    return w


def _rope_tables(pos):
    half = MLA_ROPE_DIM // 2
    inv_freq = ROPE_THETA ** (-jnp.arange(half, dtype=F32) / half)
    ang = pos.astype(F32)[:, None] * inv_freq[None, :]
    cos, sin = jnp.cos(ang), jnp.sin(ang)
    cos_t = jnp.tile(jnp.concatenate([cos, cos], axis=1), (1, MLA_HEADS))
    sin_t = jnp.tile(jnp.concatenate([-sin, sin], axis=1), (1, MLA_HEADS))
    return cos_t, sin_t


def kernel(x_prompt, x_sample, mem_prompt, cache_fox_k, cache_fox_v, cache_fox_logf, cache_mla_ckv, cache_mla_krope, cache_mem_k, cache_mem_v, page_table, g_ffn1_pre, w_ffn1_gu, w_ffn1_down, g_ffn1_post, g_mix_pre, w_in, b_fgate, g_q_norm, w_q_up, g_kv_norm, w_kv_up, w_out, g_mix_post, g_mem_tok, w_mem_kv, g_mem_pre, w_mem_q, w_mem_o, g_mem_post, g_ffn2_pre, w_ffn2_gu, w_ffn2_down, g_ffn2_post):
    n_seq, seq_len, _ = x_prompt.shape
    n_dec, dec_seq, _ = x_sample.shape
    depth = w_in.shape[0]
    n_pages = page_table.shape[1]
    past_len = n_pages * PAGE_SIZE
    n_mem = mem_prompt.shape[1]
    assert dec_seq == 4

    xp = x_prompt.reshape(n_seq * seq_len, D_MODEL)
    xs = x_sample.reshape(n_dec * dec_seq, D_MODEL)
    mem = mem_prompt.reshape(n_seq * n_mem, D_MODEL)
    cos_p, sin_p = _rope_tables(jnp.arange(seq_len, dtype=jnp.int32))
    pos_s = past_len + jnp.arange(dec_seq, dtype=jnp.int32)
    cos_s, sin_s = _rope_tables(jnp.tile(pos_s, n_dec))

    outs = {k: [] for k in ("pk", "pv", "plf", "pckv", "pkr", "pmk", "pmv",
                            "sk", "sv", "slf", "sckv", "skr")}
    for l in range(depth):
        w = _prep_weights(l, g_ffn1_pre, w_ffn1_gu, w_ffn1_down, g_ffn1_post, g_mix_pre,
                          w_in, b_fgate, g_q_norm, w_q_up, g_kv_norm, w_kv_up, w_out,
                          g_mix_post, g_mem_tok, w_mem_kv, g_mem_pre, w_mem_q, w_mem_o,
                          g_mem_post, g_ffn2_pre, w_ffn2_gu, w_ffn2_down, g_ffn2_post)

        xp = _ffn_half(xp, *w["ffn1"])
        pp = _mix_proj(xp, seq_len, w, cos_p, sin_p, prompt=True)
        o_fox = _fox_prompt(pp, n_seq, seq_len)
        o_lat = _mla_prompt(pp, n_seq, seq_len)
        xp, qm = _mix_out(xp, o_fox, o_lat, w)
        mk, mv = _memory_kv(mem, w["g_mem_tok"], w["w_mem_kv"])
        om = _mem_prompt(qm, mk.reshape(n_seq, n_mem, MEM_W), mv.reshape(n_seq, n_mem, MEM_W),
                         seq_len)
        xp = _memout_ffn(xp, om, w["w_mem_o"], w["g_mem_post"], *w["ffn2"])
        heads_last = lambda a: jnp.transpose(
            a.reshape(n_seq, FOX_KV_HEADS, FOX_HEAD_DIM, seq_len), (0, 3, 1, 2))
        outs["pk"].append(heads_last(pp["kT32"]))
        outs["pv"].append(heads_last(pp["vT32"]))
        outs["plf"].append(jnp.transpose(pp["lfT"], (0, 2, 1)))
        outs["pckv"].append(pp["ckv32"].reshape(n_seq, seq_len, MLA_KV_LORA))
        outs["pkr"].append(jnp.transpose(pp["krT32"], (0, 2, 1)))
        outs["pmk"].append(mk.reshape(n_seq, n_mem, MEM_HEADS, MEM_HEAD_DIM))
        outs["pmv"].append(mv.reshape(n_seq, n_mem, MEM_HEADS, MEM_HEAD_DIM))

        xs = _ffn_half(xs, *w["ffn1"])
        ps = _mix_proj(xs, n_dec * dec_seq, w, cos_s, sin_s, prompt=False)
        n_pool = cache_fox_k.shape[1]
        kc = jnp.transpose(cache_fox_k[l], (0, 2, 3, 1)).reshape(n_pool, FOX_KV_W, PAGE_SIZE)
        vc = jnp.transpose(cache_fox_v[l], (0, 2, 3, 1)).reshape(n_pool, FOX_KV_W, PAGE_SIZE)
        krc = jnp.transpose(cache_mla_krope[l], (0, 2, 1))
        lfc = jnp.transpose(cache_fox_logf[l], (0, 2, 1))
        o_fox, o_lat = _decode_attention(ps, page_table, kc, vc, cache_mla_ckv[l], krc, lfc)
        xs, qm = _mix_out(xs, o_fox, o_lat, w)
        om = _mem_decode(qm, cache_mem_k[l], cache_mem_v[l])
        xs = _memout_ffn(xs, om, w["w_mem_o"], w["g_mem_post"], *w["ffn2"])
        outs["sk"].append(ps["k32"].reshape(n_dec, dec_seq, FOX_KV_HEADS, FOX_HEAD_DIM))
        outs["sv"].append(ps["v32"].reshape(n_dec, dec_seq, FOX_KV_HEADS, FOX_HEAD_DIM))
        outs["slf"].append(ps["lf"].reshape(n_dec, dec_seq, FOX_HEADS))
        outs["sckv"].append(ps["ckv32"].reshape(n_dec, dec_seq, MLA_KV_LORA))
        outs["skr"].append(ps["kr32"].reshape(n_dec, dec_seq, MLA_ROPE_DIM))

    st = {k: jnp.stack(v) for k, v in outs.items()}
    return (xp.reshape(n_seq, seq_len, D_MODEL), xs.reshape(n_dec, dec_seq, D_MODEL),
            st["pk"], st["pv"], st["plf"], st["pckv"], st["pkr"], st["pmk"], st["pmv"],
            st["sk"], st["sv"], st["slf"], st["sckv"], st["skr"])
```

```python
import functools

import numpy as np
import jax
import jax.numpy as jnp
from jax import lax
from jax.experimental import pallas as pl
from jax.experimental.pallas import tpu as pltpu

BF = jnp.bfloat16
F32 = jnp.float32

D_MODEL = 1024
PAGE_SIZE = 128
FOX_HEADS = 8
FOX_KV_HEADS = 4
FOX_GROUP = FOX_HEADS // FOX_KV_HEADS
FOX_HEAD_DIM = 64
FOX_Q_W = FOX_HEADS * FOX_HEAD_DIM
FOX_KV_W = FOX_KV_HEADS * FOX_HEAD_DIM
MLA_HEADS = 4
MLA_Q_LORA = 256
MLA_KV_LORA = 256
MLA_NOPE_DIM = 128
MLA_ROPE_DIM = 64
MLA_V_DIM = 128
ROPE_THETA = 10000.0
MEM_HEADS = 4
MEM_HEAD_DIM = 128
MEM_W = MEM_HEADS * MEM_HEAD_DIM
D_FF = 2816
RMS_EPS = 1e-6
NEG_INF = -1e30

LANES = 128
VMEM_LIMIT_BYTES = 56 * 1024 * 1024

LOG2E = 1.4426950408889634
FOX_SCALE = FOX_HEAD_DIM ** -0.5 * LOG2E
MLA_SCALE = (MLA_NOPE_DIM + MLA_ROPE_DIM) ** -0.5 * LOG2E
MEM_SCALE = MEM_HEAD_DIM ** -0.5

ROW_TILE = 512
FF_CHUNK = 256
ATT_TQ = 512
ATT_TK = 1024
DEC_PAGES = 32
MEM_DEC_BATCH = 8


def _params(*sem):
    return pltpu.CompilerParams(dimension_semantics=sem,
                                vmem_limit_bytes=VMEM_LIMIT_BYTES)


def _rms(x, g):
    return x * lax.rsqrt(jnp.mean(x * x, axis=-1, keepdims=True) + RMS_EPS) * g


def _dot(a, b):
    return jnp.dot(a, b, preferred_element_type=F32)


def _dot_nt(a, b):
    return lax.dot_general(a, b, (((1,), (1,)), ((), ())), preferred_element_type=F32)


def _const_spec(shape):
    zeros = (0,) * len(shape)
    return pl.BlockSpec(shape, lambda *_: zeros)


def _row_tile(n):
    return ROW_TILE if n % ROW_TILE == 0 else n


def _swiglu_half(x, gpre, wgu_ref, wd_ref, gpost):
    h = _rms(x, gpre).astype(BF)
    acc = jnp.zeros(x.shape, F32)
    for c in range(D_FF // FF_CHUNK):
        g = _dot(h, wgu_ref[:, c * FF_CHUNK:(c + 1) * FF_CHUNK])
        u = _dot(h, wgu_ref[:, D_FF + c * FF_CHUNK:D_FF + (c + 1) * FF_CHUNK])
        a = (g * jax.nn.sigmoid(g) * u).astype(BF)
        acc = acc + _dot(a, wd_ref[c * FF_CHUNK:(c + 1) * FF_CHUNK, :])
    return x + 0.5 * _rms(acc, gpost)


def _ffn_kernel(x_ref, gpre_ref, wgu_ref, wd_ref, gpost_ref, o_ref):
    o_ref[...] = _swiglu_half(x_ref[...], gpre_ref[...], wgu_ref, wd_ref, gpost_ref[...])


def _ffn_half(x, gpre, wgu, wd, gpost):
    n = x.shape[0]
    tm = _row_tile(n)
    row = pl.BlockSpec((tm, D_MODEL), lambda i: (i, 0))
    return pl.pallas_call(
        _ffn_kernel,
        grid=(n // tm,),
        in_specs=[row, _const_spec((1, D_MODEL)), _const_spec((D_MODEL, 2 * D_FF)),
                  _const_spec((D_FF, D_MODEL)), _const_spec((1, D_MODEL))],
        out_specs=row,
        out_shape=jax.ShapeDtypeStruct((n, D_MODEL), F32),
        compiler_params=_params("parallel"),
        name="ffn_half",
    )(x, gpre, wgu, wd, gpost)


def _memout_ffn_kernel(x_ref, om_ref, wmo_ref, gmpost_ref, gpre_ref, wgu_ref, wd_ref, gpost_ref,
                       o_ref):
    x = x_ref[...] + _rms(_dot(om_ref[...], wmo_ref[...]), gmpost_ref[...])
    o_ref[...] = _swiglu_half(x, gpre_ref[...], wgu_ref, wd_ref, gpost_ref[...])


def _memout_ffn(x, om, wmo, gmpost, gpre, wgu, wd, gpost):
    n = x.shape[0]
    tm = _row_tile(n)
    row = pl.BlockSpec((tm, D_MODEL), lambda i: (i, 0))
    return pl.pallas_call(
        _memout_ffn_kernel,
        grid=(n // tm,),
        in_specs=[row, pl.BlockSpec((tm, MEM_W), lambda i: (i, 0)),
                  _const_spec((MEM_W, D_MODEL)), _const_spec((1, D_MODEL)),
                  _const_spec((1, D_MODEL)), _const_spec((D_MODEL, 2 * D_FF)),
                  _const_spec((D_FF, D_MODEL)), _const_spec((1, D_MODEL))],
        out_specs=row,
        out_shape=jax.ShapeDtypeStruct((n, D_MODEL), F32),
        compiler_params=_params("parallel"),
        name="memout_ffn",
    )(x, om, wmo, gmpost, gpre, wgu, wd, gpost)


BIG_W = FOX_Q_W + 2 * FOX_KV_W + MLA_Q_LORA + MLA_KV_LORA
SMALL_W = 3 * LANES


def _log_sigmoid(x):
    return jnp.minimum(x, 0.0) - jnp.log1p(jnp.exp(-jnp.abs(x)))


MIX_COMMON = ("qlat", "qrope", "ckv32")
MIX_PROMPT = ("kT32", "vT32", "krT32", "lfT", "qT", "k16", "vT1", "ckv16", "ckvT", "krT", "c", "cT")
MIX_DECODE = ("q", "k32", "v32", "kr32", "lf")
FOX_VT_ROWS = FOX_HEAD_DIM + 16


def _mixproj_kernel(x_ref, g_ref, wbig_ref, wsmall_ref, bf_ref, gq_ref, gkv_ref,
                    wqn_ref, wqra_ref, wqrb_ref, wuk_ref, cos_ref, sin_ref, *rest,
                    names, tiles_per_seq):
    o = dict(zip(names, rest))
    carry_ref = rest[-1]
    prompt = "qT" in o
    i = pl.program_id(0)
    tm = x_ref.shape[0]
    h = _rms(x_ref[...], g_ref[...]).astype(BF)
    big = _dot(h, wbig_ref[...])
    small = _dot(h, wsmall_ref[...])
    cos = cos_ref[...]
    sin = sin_ref[...]

    q = big[:, :FOX_Q_W] * FOX_SCALE
    k = big[:, FOX_Q_W:FOX_Q_W + FOX_KV_W]
    v = big[:, FOX_Q_W + FOX_KV_W:FOX_Q_W + 2 * FOX_KV_W]
    lf = _log_sigmoid(small[:, 2 * LANES:] + bf_ref[...])
    if prompt:
        o["kT32"][0] = k.T
        o["k16"][...] = k.astype(BF)
        vT = v.T
        o["vT32"][0] = vT
        ones_row = (lax.broadcasted_iota(jnp.int32, (FOX_VT_ROWS - FOX_HEAD_DIM, tm), 0) == 0).astype(BF)
        for kvh in range(FOX_KV_HEADS):
            o["vT1"][0, kvh * FOX_VT_ROWS:kvh * FOX_VT_ROWS + FOX_HEAD_DIM] = (
                vT[kvh * FOX_HEAD_DIM:(kvh + 1) * FOX_HEAD_DIM].astype(BF))
            o["vT1"][0, kvh * FOX_VT_ROWS + FOX_HEAD_DIM:(kvh + 1) * FOX_VT_ROWS] = ones_row
        qT = q.T.astype(BF)
        zero = jnp.zeros((FOX_HEAD_DIM, tm), BF)
        for hh in range(FOX_HEADS):
            piece = qT[hh * FOX_HEAD_DIM:(hh + 1) * FOX_HEAD_DIM]
            odd = (hh // FOX_GROUP) % 2 == 1
            o["qT"][0, hh * LANES:(hh + 1) * LANES] = jnp.concatenate(
                [zero, piece] if odd else [piece, zero], axis=0)

        lfT = lf.T
        o["lfT"][0] = lfT[:FOX_HEADS]

        @pl.when(i % tiles_per_seq == 0)
        def _():
            carry_ref[...] = jnp.zeros_like(carry_ref)

        lane = lax.broadcasted_iota(jnp.int32, lfT.shape, 1)
        run = lfT
        sh = 1
        while sh < tm:
            run = run + jnp.where(lane >= sh, pltpu.roll(run, sh, axis=1), 0.0)
            sh *= 2
        run = run + carry_ref[...]
        carry_ref[...] = run[:, tm - 1:tm]
        run2 = run * LOG2E
        o["cT"][0] = run2[:FOX_HEADS]
        o["c"][...] = run2.T[:, :FOX_HEADS]
    else:
        o["q"][...] = q.astype(BF)
        o["k32"][...] = k
        o["v32"][...] = v
        o["lf"][...] = lf[:, :FOX_HEADS]

    cq = big[:, FOX_Q_W + 2 * FOX_KV_W:FOX_Q_W + 2 * FOX_KV_W + MLA_Q_LORA]
    cqn = _rms(cq, gq_ref[...]).astype(BF)
    qn = _dot(cqn, wqn_ref[...]).astype(BF)
    for hd in range(MLA_HEADS):
        ql = _dot(qn[:, hd * MLA_NOPE_DIM:(hd + 1) * MLA_NOPE_DIM], wuk_ref[hd])
        o["qlat"][:, hd * MLA_KV_LORA:(hd + 1) * MLA_KV_LORA] = (ql * MLA_SCALE).astype(BF)
    qr = _dot(cqn, wqra_ref[...]) * cos + _dot(cqn, wqrb_ref[...]) * sin
    o["qrope"][...] = (qr * MLA_SCALE).astype(BF)

    ckv = _rms(big[:, BIG_W - MLA_KV_LORA:], gkv_ref[...])
    o["ckv32"][...] = ckv
    kr = small[:, :LANES] * cos[:, :LANES] + small[:, LANES:2 * LANES] * sin[:, :LANES]
    if prompt:
        o["ckv16"][...] = ckv.astype(BF)
        o["ckvT"][0] = ckv.T.astype(BF)
        krT = kr.T[:MLA_ROPE_DIM]
        o["krT32"][0] = krT
        o["krT"][0] = krT.astype(BF)
    else:
        o["kr32"][...] = kr[:, :MLA_ROPE_DIM]


def _mix_proj(x, seq_len, w, cos_tab, sin_tab, prompt):
    n = x.shape[0]
    tm = _row_tile(min(n, seq_len))
    n_seq = n // seq_len
    tps = seq_len // tm
    tab_tiles = cos_tab.shape[0] // tm
    row = lambda width: pl.BlockSpec((tm, width), lambda i: (i, 0))
    colT = lambda height: pl.BlockSpec((1, height, tm), lambda i: (i // tps, 0, i % tps))
    tab = pl.BlockSpec((tm, 2 * LANES), lambda i: (i % tab_tiles, 0))
    rows = lambda width, dt: (jax.ShapeDtypeStruct((n, width), dt), row(width))
    cols = lambda height, dt: (jax.ShapeDtypeStruct((n_seq, height, seq_len), dt), colT(height))
    outputs = {
        "q": rows(FOX_Q_W, BF), "qlat": rows(MLA_HEADS * MLA_KV_LORA, BF),
        "qrope": rows(MLA_HEADS * MLA_ROPE_DIM, BF), "ckv32": rows(MLA_KV_LORA, F32),
        "kT32": cols(FOX_KV_W, F32), "vT32": cols(FOX_KV_W, F32),
        "krT32": cols(MLA_ROPE_DIM, F32), "lfT": cols(FOX_HEADS, F32),
        "qT": cols(FOX_HEADS * LANES, BF), "k16": rows(FOX_KV_W, BF),
        "vT1": cols(FOX_KV_HEADS * FOX_VT_ROWS, BF),
        "ckv16": rows(MLA_KV_LORA, BF), "ckvT": cols(MLA_KV_LORA, BF),
        "krT": cols(MLA_ROPE_DIM, BF), "c": rows(FOX_HEADS, F32), "cT": cols(FOX_HEADS, F32),
        "k32": rows(FOX_KV_W, F32), "v32": rows(FOX_KV_W, F32),
        "kr32": rows(MLA_ROPE_DIM, F32), "lf": rows(FOX_HEADS, F32),
    }
    names = MIX_COMMON + (MIX_PROMPT if prompt else MIX_DECODE)
    out_shape = tuple(outputs[k][0] for k in names)
    out_specs = tuple(outputs[k][1] for k in names)
    in_specs = [
        row(D_MODEL), _const_spec((1, D_MODEL)), _const_spec((D_MODEL, BIG_W)),
        _const_spec((D_MODEL, SMALL_W)), _const_spec((1, LANES)),
        _const_spec((1, MLA_Q_LORA)), _const_spec((1, MLA_KV_LORA)),
        _const_spec((MLA_Q_LORA, MLA_HEADS * MLA_NOPE_DIM)),
        _const_spec((MLA_Q_LORA, MLA_HEADS * MLA_ROPE_DIM)),
        _const_spec((MLA_Q_LORA, MLA_HEADS * MLA_ROPE_DIM)),
        _const_spec((MLA_HEADS, MLA_NOPE_DIM, MLA_KV_LORA)), tab, tab,
    ]
    outs = pl.pallas_call(
        functools.partial(_mixproj_kernel, names=names, tiles_per_seq=tps),
        grid=(n // tm,),
        in_specs=in_specs,
        out_specs=out_specs,
        out_shape=out_shape,
        scratch_shapes=[pltpu.VMEM((LANES, 1), F32)],
        compiler_params=_params("arbitrary"),
        name="mix_proj",
    )(x, w["g_mix_pre"], w["w_big"], w["w_small"], w["b_f"], w["g_q_norm"],
      w["g_kv_norm"], w["w_q_nope"], w["w_q_rope_a"], w["w_q_rope_b"], w["w_uk_t"],
      cos_tab, sin_tab)
    return dict(zip(names, outs))


def _online_update(m_ref, l_ref, acc_ref, h, z, v, row_bias=None):
    m_prev = m_ref[h]
    m_z = jnp.max(z, axis=1, keepdims=True)
    if row_bias is None:
        m_new = jnp.maximum(m_prev, m_z)
        shift = m_new
    else:
        m_new = jnp.maximum(m_prev, m_z + row_bias)
        shift = m_new - row_bias
    alpha = jnp.exp2(m_prev - m_new)
    p = jnp.exp2(z - shift)
    if l_ref is not None:
        l_ref[h] = alpha * l_ref[h] + jnp.sum(p, axis=1, keepdims=True)
    acc_ref[h] = alpha * acc_ref[h] + _dot(p.astype(BF), v)
    m_ref[h] = m_new


def _init_state(m_ref, l_ref, acc_ref):
    m_ref[...] = jnp.full_like(m_ref, NEG_INF)
    if l_ref is not None:
        l_ref[...] = jnp.zeros_like(l_ref)
    acc_ref[...] = jnp.zeros_like(acc_ref)


def _causal_keep(shape, q0, k0, q_axis):
    qpos = q0 + lax.broadcasted_iota(jnp.int32, shape, q_axis)
    kpos = k0 + lax.broadcasted_iota(jnp.int32, shape, 1 - q_axis)
    return qpos >= kpos


def _sweep_key_blocks(block, i, tq, q_axis=0):
    assert ATT_TK % tq == 0
    n_full = (i * tq) // ATT_TK

    def body(kj, carry):
        block(pl.multiple_of(kj * ATT_TK, ATT_TK), None)
        return carry

    lax.fori_loop(0, n_full, body, 0)
    k0 = pl.multiple_of(n_full * ATT_TK, ATT_TK)
    shape = (tq, ATT_TK) if q_axis == 0 else (ATT_TK, tq)
    block(k0, _causal_keep(shape, i * tq, k0, q_axis))


def _fox_prompt_kernel(qT_ref, cq_ref, k_ref, vT1_ref, ck_ref, o_ref, m_ref, acc_ref):
    i = pl.program_id(1)
    tq = qT_ref.shape[2]
    _init_state(m_ref, None, acc_ref)

    def block(k0, keep):
        keys = pl.ds(k0, ATT_TK)

        def scores(hh):
            pair = hh // (2 * FOX_GROUP)
            kq = _dot(k_ref[keys, pair * LANES:(pair + 1) * LANES], qT_ref[0, hh * LANES:(hh + 1) * LANES, :])
            return kq - ck_ref[keys, hh:hh + 1]

        z = scores(0)
        for hh in range(FOX_HEADS):
            z_next = scores(hh + 1) if hh + 1 < FOX_HEADS else None
            if keep is not None:
                z = jnp.where(keep, z, NEG_INF)
            kvh = hh // FOX_GROUP
            cq = cq_ref[0, hh:hh + 1, :]
            m_prev = m_ref[hh]
            m_new = jnp.maximum(m_prev, jnp.max(z, axis=0, keepdims=True) + cq)
            alpha = jnp.exp2(m_prev - m_new)
            p = jnp.exp2(z - (m_new - cq))
            vT1 = vT1_ref[0, kvh * FOX_VT_ROWS:(kvh + 1) * FOX_VT_ROWS, keys]
            acc_ref[hh] = alpha * acc_ref[hh] + _dot(vT1, p.astype(BF))
            m_ref[hh] = m_new
            z = z_next

    _sweep_key_blocks(block, i, tq, q_axis=1)
    for hh in range(FOX_HEADS):
        acc = acc_ref[hh]
        oT = acc[:FOX_HEAD_DIM] / acc[FOX_HEAD_DIM:FOX_HEAD_DIM + 1]
        o_ref[:, hh * FOX_HEAD_DIM:(hh + 1) * FOX_HEAD_DIM] = oT.T.astype(BF)


def _fox_prompt(p, n_seq, seq_len):
    tq = ATT_TQ
    nq = seq_len // tq
    return pl.pallas_call(
        _fox_prompt_kernel,
        grid=(n_seq, nq),
        in_specs=[
            pl.BlockSpec((1, FOX_HEADS * LANES, tq), lambda b, i: (b, 0, i)),
            pl.BlockSpec((1, FOX_HEADS, tq), lambda b, i: (b, 0, i)),
            pl.BlockSpec((seq_len, FOX_KV_W), lambda b, i: (b, 0)),
            pl.BlockSpec((1, FOX_KV_HEADS * FOX_VT_ROWS, seq_len), lambda b, i: (b, 0, 0)),
            pl.BlockSpec((seq_len, FOX_HEADS), lambda b, i: (b, 0)),
        ],
        out_specs=pl.BlockSpec((tq, FOX_Q_W), lambda b, i: (b * nq + i, 0)),
        out_shape=jax.ShapeDtypeStruct((n_seq * seq_len, FOX_Q_W), BF),
        scratch_shapes=[pltpu.VMEM((FOX_HEADS, 1, tq), F32),
                        pltpu.VMEM((FOX_HEADS, FOX_VT_ROWS, tq), F32)],
        compiler_params=_params("parallel", "arbitrary"),
        name="fox_prompt",
    )(p["qT"], p["cT"], p["k16"], p["vT1"], p["c"])


def _mla_prompt_kernel(ql_ref, qr_ref, ckvT_ref, krT_ref, ckv_ref, o_ref, qrs_ref, m_ref, l_ref,
                       acc_ref):
    i = pl.program_id(1)
    tq = ql_ref.shape[0]
    for hd in range(MLA_HEADS):
        qrs_ref[hd] = qr_ref[:, hd * MLA_ROPE_DIM:(hd + 1) * MLA_ROPE_DIM]
    _init_state(m_ref, l_ref, acc_ref)

    def block(k0, keep):
        keys = pl.ds(k0, ATT_TK)
        ckvT = ckvT_ref[0, :, keys]
        krT = krT_ref[0, :, keys]
        ckv = ckv_ref[keys, :]

        def scores(hd):
            return (_dot(ql_ref[:, hd * MLA_KV_LORA:(hd + 1) * MLA_KV_LORA], ckvT)
                    + _dot(qrs_ref[hd], krT))

        z = scores(0)
        for hd in range(MLA_HEADS):
            z_next = scores(hd + 1) if hd + 1 < MLA_HEADS else None
            if keep is not None:
                z = jnp.where(keep, z, NEG_INF)
            _online_update(m_ref, l_ref, acc_ref, hd, z, ckv)
            z = z_next

    _sweep_key_blocks(block, i, tq)
    for hd in range(MLA_HEADS):
        o_ref[:, hd * MLA_KV_LORA:(hd + 1) * MLA_KV_LORA] = (acc_ref[hd] / l_ref[hd]).astype(BF)


def _mla_prompt(p, n_seq, seq_len):
    tq = ATT_TQ
    nq = seq_len // tq
    lat_w = MLA_HEADS * MLA_KV_LORA
    return pl.pallas_call(
        _mla_prompt_kernel,
        grid=(n_seq, nq),
        in_specs=[
            pl.BlockSpec((tq, lat_w), lambda b, i: (b * nq + i, 0)),
            pl.BlockSpec((tq, MLA_HEADS * MLA_ROPE_DIM), lambda b, i: (b * nq + i, 0)),
            pl.BlockSpec((1, MLA_KV_LORA, seq_len), lambda b, i: (b, 0, 0)),
            pl.BlockSpec((1, MLA_ROPE_DIM, seq_len), lambda b, i: (b, 0, 0)),
            pl.BlockSpec((seq_len, MLA_KV_LORA), lambda b, i: (b, 0)),
        ],
        out_specs=pl.BlockSpec((tq, lat_w), lambda b, i: (b * nq + i, 0)),
        out_shape=jax.ShapeDtypeStruct((n_seq * seq_len, lat_w), BF),
        scratch_shapes=[pltpu.VMEM((MLA_HEADS, tq, MLA_ROPE_DIM), BF),
                        pltpu.VMEM((MLA_HEADS, tq, 1), F32),
                        pltpu.VMEM((MLA_HEADS, tq, 1), F32),
                        pltpu.VMEM((MLA_HEADS, tq, MLA_KV_LORA), F32)],
        compiler_params=_params("parallel", "arbitrary"),
        name="mla_prompt",
    )(p["qlat"], p["qrope"], p["ckvT"], p["krT"], p["ckv16"])


DEC_FOX_ROWS = 4 * FOX_HEADS
DEC_MLA_ROWS = 4 * MLA_HEADS


def _decode_kernel(pt_ref, qbd_ref, lfnew_ref, knew_ref, vnew_ref, qlat_ref, qrope_ref,
                   ckvnew_ref, krnew_ref, kc_ref, vc_ref, ckvc_ref, krc_ref, lfc_ref,
                   of_ref, om_ref,
                   kbuf, vbuf, ckvbuf, krbuf, lfbuf, sems,
                   mf_ref, lf_ref, accf_ref, mm_ref, lm_ref, accm_ref, rcarry_ref,
                   *, n_chunks, n_steps):
    t = pl.program_id(0)
    n_keys = DEC_PAGES * PAGE_SIZE

    def chunk_copies(step, slot):
        b = step // n_chunks
        first = (n_chunks - 1 - step % n_chunks) * DEC_PAGES
        copies = []
        for j in range(DEC_PAGES):
            pid = pt_ref[b, first + j]
            lanes = pl.ds(j * PAGE_SIZE, PAGE_SIZE)
            copies += [
                pltpu.make_async_copy(kc_ref.at[pid], kbuf.at[slot, :, lanes], sems.at[slot, 0]),
                pltpu.make_async_copy(vc_ref.at[pid], vbuf.at[slot, :, lanes], sems.at[slot, 1]),
                pltpu.make_async_copy(ckvc_ref.at[pid], ckvbuf.at[slot, lanes, :], sems.at[slot, 2]),
                pltpu.make_async_copy(krc_ref.at[pid], krbuf.at[slot, :, lanes], sems.at[slot, 3]),
                pltpu.make_async_copy(lfc_ref.at[pid], lfbuf.at[slot, :, lanes], sems.at[slot, 4]),
            ]
        return copies

    slot = t % 2

    @pl.when(t == 0)
    def _():
        for cp in chunk_copies(t, slot):
            cp.start()

    @pl.when(t + 1 < n_steps)
    def _():
        for cp in chunk_copies(t + 1, 1 - slot):
            cp.start()

    @pl.when(t % n_chunks == 0)
    def _():
        mf_ref[...] = jnp.full_like(mf_ref, NEG_INF)
        lf_ref[...] = jnp.zeros_like(lf_ref)
        accf_ref[...] = jnp.zeros_like(accf_ref)
        mm_ref[...] = jnp.full_like(mm_ref, NEG_INF)
        lm_ref[...] = jnp.zeros_like(lm_ref)
        accm_ref[...] = jnp.zeros_like(accm_ref)
        rcarry_ref[...] = jnp.zeros_like(rcarry_ref)

    for cp in chunk_copies(t, slot):
        cp.wait()

    lfnew = lfnew_ref[0]
    parts = [lfnew[0:FOX_HEADS]]
    for u in range(1, 4):
        parts.append(parts[-1] + lfnew[u * FOX_HEADS:(u + 1) * FOX_HEADS])
    ncol = jnp.concatenate(parts, axis=0) * LOG2E
    qbd = qbd_ref[0]

    lfp = lfbuf[slot]
    lane = lax.broadcasted_iota(jnp.int32, lfp.shape, 1)
    run = lfp
    sh = 1
    while sh < n_keys:
        run = run + jnp.where(lane < n_keys - sh, pltpu.roll(run, n_keys - sh, axis=1), 0.0)
        sh *= 2
    later = run - lfp + rcarry_ref[...]
    rcarry_ref[...] = rcarry_ref[...] + run[:, 0:1]
    bias = jnp.concatenate([later * LOG2E] * 4, axis=0) + ncol

    def update(m_ref, l_ref, acc_ref, s, pv_fn):
        m_prev = m_ref[...]
        m_new = jnp.maximum(m_prev, jnp.max(s, axis=1, keepdims=True))
        alpha = jnp.exp2(m_prev - m_new)
        p = jnp.exp2(s - m_new)
        l_ref[...] = alpha * l_ref[...] + jnp.sum(p, axis=1, keepdims=True)
        acc_ref[...] = alpha * acc_ref[...] + pv_fn(p)
        m_ref[...] = m_new

    s_f = _dot(qbd, kbuf[slot].astype(BF)) + bias
    ckv = ckvbuf[slot].astype(BF)
    qlat = qlat_ref[0]
    qrope = qrope_ref[0]
    s_m = _dot_nt(qlat, ckv) + _dot(qrope, krbuf[slot].astype(BF))
    update(mf_ref, lf_ref, accf_ref, s_f,
           lambda p: _dot_nt(p.astype(BF), vbuf[slot].astype(BF)))
    update(mm_ref, lm_ref, accm_ref, s_m, lambda p: _dot(p.astype(BF), ckv))

    @pl.when(t % n_chunks == n_chunks - 1)
    def _():
        qf = qbd.astype(F32)
        rowf = lax.broadcasted_iota(jnp.int32, (DEC_FOX_ROWS, 1), 0)
        knew = knew_ref[0]
        vnew = vnew_ref[0]
        for u in range(4):
            s = jnp.sum(qf * knew[u:u + 1, :], axis=1, keepdims=True)
            n_u = jnp.concatenate([ncol[u * FOX_HEADS:(u + 1) * FOX_HEADS]] * 4, axis=0)
            s = jnp.where(rowf >= u * FOX_HEADS, s + (ncol - n_u), NEG_INF)
            update(mf_ref, lf_ref, accf_ref, s, lambda p: p * vnew[u:u + 1, :])
        o = accf_ref[...] / lf_ref[...]
        row = lax.broadcasted_iota(jnp.int32, o.shape, 0)
        col = lax.broadcasted_iota(jnp.int32, o.shape, 1)
        o = jnp.where(col // FOX_HEAD_DIM == (row % FOX_HEADS) // FOX_GROUP, o, 0.0)
        of_ref[0] = (o[:, 0:64] + o[:, 64:128]) + (o[:, 128:192] + o[:, 192:256])

        qlf = qlat.astype(F32)
        qrf = qrope.astype(F32)
        rowm = lax.broadcasted_iota(jnp.int32, (DEC_MLA_ROWS, 1), 0)
        ckvnew = ckvnew_ref[0]
        krnew = krnew_ref[0]
        for u in range(4):
            s = (jnp.sum(qlf * ckvnew[u:u + 1, :], axis=1, keepdims=True)
                 + jnp.sum(qrf * krnew[u:u + 1, :], axis=1, keepdims=True))
            s = jnp.where(rowm >= u * MLA_HEADS, s, NEG_INF)
            update(mm_ref, lm_ref, accm_ref, s, lambda p: p * ckvnew[u:u + 1, :])
        om_ref[0] = accm_ref[...] / lm_ref[...]


def _decode_attention(ps, page_table, kc, vc, ckvc, krc, lfc):
    n_b, n_pages = page_table.shape
    n_chunks = n_pages // DEC_PAGES
    n_steps = n_b * n_chunks
    n_keys = DEC_PAGES * PAGE_SIZE

    q = ps["q"].reshape(n_b, 4, FOX_KV_HEADS, FOX_GROUP, FOX_HEAD_DIM)
    eye = jnp.eye(FOX_KV_HEADS, dtype=BF)
    qbd = jnp.einsum("btkgd,kj->btkgjd", q, eye).reshape(n_b, DEC_FOX_ROWS, FOX_KV_W)
    lfnew = ps["lf"].reshape(n_b, DEC_FOX_ROWS, 1)
    knew = ps["k32"].reshape(n_b, 4, FOX_KV_W)
    vnew = ps["v32"].reshape(n_b, 4, FOX_KV_W)
    qlat = ps["qlat"].reshape(n_b, DEC_MLA_ROWS, MLA_KV_LORA)
    qrope = ps["qrope"].reshape(n_b, DEC_MLA_ROWS, MLA_ROPE_DIM)
    ckvnew = ps["ckv32"].reshape(n_b, 4, MLA_KV_LORA)
    krnew = ps["kr32"].reshape(n_b, 4, MLA_ROPE_DIM)

    per_b = lambda r, w: pl.BlockSpec((1, r, w), lambda t, pt: (t // n_chunks, 0, 0))
    hbm = pl.BlockSpec(memory_space=pl.ANY)
    grid_spec = pltpu.PrefetchScalarGridSpec(
        num_scalar_prefetch=1,
        grid=(n_steps,),
        in_specs=[per_b(DEC_FOX_ROWS, FOX_KV_W), per_b(DEC_FOX_ROWS, 1),
                  per_b(4, FOX_KV_W), per_b(4, FOX_KV_W),
                  per_b(DEC_MLA_ROWS, MLA_KV_LORA), per_b(DEC_MLA_ROWS, MLA_ROPE_DIM),
                  per_b(4, MLA_KV_LORA), per_b(4, MLA_ROPE_DIM),
                  hbm, hbm, hbm, hbm, hbm],
        out_specs=[per_b(DEC_FOX_ROWS, FOX_HEAD_DIM), per_b(DEC_MLA_ROWS, MLA_KV_LORA)],
        scratch_shapes=[
            pltpu.VMEM((2, FOX_KV_W, n_keys), F32),
            pltpu.VMEM((2, FOX_KV_W, n_keys), F32),
            pltpu.VMEM((2, n_keys, MLA_KV_LORA), F32),
            pltpu.VMEM((2, MLA_ROPE_DIM, n_keys), F32),
            pltpu.VMEM((2, FOX_HEADS, n_keys), F32),
            pltpu.SemaphoreType.DMA((2, 5)),
            pltpu.VMEM((DEC_FOX_ROWS, 1), F32), pltpu.VMEM((DEC_FOX_ROWS, 1), F32),
            pltpu.VMEM((DEC_FOX_ROWS, FOX_KV_W), F32),
            pltpu.VMEM((DEC_MLA_ROWS, 1), F32), pltpu.VMEM((DEC_MLA_ROWS, 1), F32),
            pltpu.VMEM((DEC_MLA_ROWS, MLA_KV_LORA), F32),
            pltpu.VMEM((FOX_HEADS, 1), F32),
        ],
    )
    o_fox, o_lat = pl.pallas_call(
        functools.partial(_decode_kernel, n_chunks=n_chunks, n_steps=n_steps),
        grid_spec=grid_spec,
        out_shape=(jax.ShapeDtypeStruct((n_b, DEC_FOX_ROWS, FOX_HEAD_DIM), F32),
                   jax.ShapeDtypeStruct((n_b, DEC_MLA_ROWS, MLA_KV_LORA), F32)),
        compiler_params=_params("arbitrary"),
        name="decode_attention",
    )(page_table, qbd, lfnew, knew, vnew, qlat, qrope, ckvnew, krnew, kc, vc, ckvc, krc, lfc)
    return (o_fox.reshape(n_b * 4, FOX_Q_W).astype(BF),
            o_lat.reshape(n_b * 4, MLA_HEADS * MLA_KV_LORA).astype(BF))


def _mixout_kernel(x_ref, of_ref, ol_ref, wuv_ref, wo_ref, gpost_ref, gmpre_ref, wmq_ref,
                   x_out_ref, qm_ref):
    parts = [of_ref[...]]
    for hd in range(MLA_HEADS):
        om = _dot(ol_ref[:, hd * MLA_KV_LORA:(hd + 1) * MLA_KV_LORA], wuv_ref[hd])
        parts.append(om.astype(BF))
    o = jnp.concatenate(parts, axis=1)
    x = x_ref[...] + _rms(_dot(o, wo_ref[...]), gpost_ref[...])
    x_out_ref[...] = x
    hm = _rms(x, gmpre_ref[...]).astype(BF)
    qm_ref[...] = (_dot(hm, wmq_ref[...]) * MEM_SCALE).astype(BF)


def _mix_out(x, o_fox, o_lat, w):
    n = x.shape[0]
    tm = _row_tile(n)
    row = lambda width: pl.BlockSpec((tm, width), lambda i: (i, 0))
    mix_w = FOX_Q_W + MLA_HEADS * MLA_V_DIM
    return pl.pallas_call(
        _mixout_kernel,
        grid=(n // tm,),
        in_specs=[row(D_MODEL), row(FOX_Q_W), row(MLA_HEADS * MLA_KV_LORA),
                  _const_spec((MLA_HEADS, MLA_KV_LORA, MLA_V_DIM)),
                  _const_spec((mix_w, D_MODEL)), _const_spec((1, D_MODEL)),
                  _const_spec((1, D_MODEL)), _const_spec((D_MODEL, MEM_W))],
        out_specs=(row(D_MODEL), row(MEM_W)),
        out_shape=(jax.ShapeDtypeStruct((n, D_MODEL), F32),
                   jax.ShapeDtypeStruct((n, MEM_W), BF)),
        compiler_params=_params("parallel"),
        name="mix_out",
    )(x, o_fox, o_lat, w["w_uv"], w["w_out"], w["g_mix_post"], w["g_mem_pre"], w["w_mem_q"])


def _memkv_kernel(mem_ref, g_ref, w_ref, k_ref, v_ref):
    kv = _dot(_rms(mem_ref[...], g_ref[...]).astype(BF), w_ref[...])
    k_ref[...] = kv[:, :MEM_W]
    v_ref[...] = kv[:, MEM_W:]


def _memory_kv(mem, g, w):
    n = mem.shape[0]
    return pl.pallas_call(
        _memkv_kernel,
        grid=(1,),
        in_specs=[_const_spec((n, D_MODEL)), _const_spec((1, D_MODEL)),
                  _const_spec((D_MODEL, 2 * MEM_W))],
        out_specs=(_const_spec((n, MEM_W)), _const_spec((n, MEM_W))),
        out_shape=(jax.ShapeDtypeStruct((n, MEM_W), F32),) * 2,
        compiler_params=_params("arbitrary"),
        name="memory_kv",
    )(mem, g, w)


def _softmax_rows(s):
    p = jnp.exp(s - jnp.max(s, axis=1, keepdims=True))
    return p, jnp.sum(p, axis=1, keepdims=True)


def _mem_prompt_kernel(q_ref, k_ref, v_ref, o_ref):
    for hd in range(MEM_HEADS):
        sl = slice(hd * MEM_HEAD_DIM, (hd + 1) * MEM_HEAD_DIM)
        p, l = _softmax_rows(_dot_nt(q_ref[:, sl], k_ref[0, :, sl].astype(BF)))
        o = _dot(p.astype(BF), v_ref[0, :, sl].astype(BF))
        o_ref[:, sl] = (o / l).astype(BF)


def _mem_prompt(qm, mk, mv, seq_len):
    n = qm.shape[0]
    tm = _row_tile(seq_len)
    tps = seq_len // tm
    n_mem = mk.shape[1]
    kv = pl.BlockSpec((1, n_mem, MEM_W), lambda i: (i // tps, 0, 0))
    return pl.pallas_call(
        _mem_prompt_kernel,
        grid=(n // tm,),
        in_specs=[pl.BlockSpec((tm, MEM_W), lambda i: (i, 0)), kv, kv],
        out_specs=pl.BlockSpec((tm, MEM_W), lambda i: (i, 0)),
        out_shape=jax.ShapeDtypeStruct((n, MEM_W), BF),
        compiler_params=_params("parallel"),
        name="mem_prompt",
    )(qm, mk, mv)


def _mem_decode_kernel(q_ref, k_ref, v_ref, o_ref):
    rows, n_kv = q_ref.shape[1], k_ref.shape[1]
    r_head = lax.broadcasted_iota(jnp.int32, (rows, n_kv), 0) & (MEM_HEADS - 1)
    c_head = lax.broadcasted_iota(jnp.int32, (rows, n_kv), 1) & (MEM_HEADS - 1)
    own = r_head == c_head
    n_b = q_ref.shape[0]
    scores = [_dot_nt(q_ref[b], k_ref[b].astype(BF)) for b in range(n_b)]
    for b in range(n_b):
        p, l = _softmax_rows(jnp.where(own, scores[b], NEG_INF))
        o_ref[b] = (_dot(p.astype(BF), v_ref[b].astype(BF)) / l).astype(BF)


def _mem_decode(qm, cache_k, cache_v):
    n_b, n_mem = cache_k.shape[:2]
    assert MEM_HEADS & (MEM_HEADS - 1) == 0
    rows = 4 * MEM_HEADS
    n_kv = n_mem * MEM_HEADS
    g = MEM_DEC_BATCH if n_b % MEM_DEC_BATCH == 0 else n_b
    kv = pl.BlockSpec((g, n_kv, MEM_HEAD_DIM), lambda i: (i, 0, 0))
    qo = pl.BlockSpec((g, rows, MEM_HEAD_DIM), lambda i: (i, 0, 0))
    o = pl.pallas_call(
        _mem_decode_kernel,
        grid=(n_b // g,),
        in_specs=[qo, kv, kv],
        out_specs=qo,
        out_shape=jax.ShapeDtypeStruct((n_b, rows, MEM_HEAD_DIM), BF),
        compiler_params=_params("parallel"),
        name="mem_decode",
    )(qm.reshape(n_b, rows, MEM_HEAD_DIM), cache_k.reshape(n_b, n_kv, MEM_HEAD_DIM),
      cache_v.reshape(n_b, n_kv, MEM_HEAD_DIM))
    return o.reshape(n_b * 4, MEM_W)


def _prep_weights(l, g_ffn1_pre, w_ffn1_gu, w_ffn1_down, g_ffn1_post, g_mix_pre, w_in,
                  b_fgate, g_q_norm, w_q_up, g_kv_norm, w_kv_up, w_out, g_mix_post,
                  g_mem_tok, w_mem_kv, g_mem_pre, w_mem_q, w_mem_o, g_mem_post,
                  g_ffn2_pre, w_ffn2_gu, w_ffn2_down, g_ffn2_post):
    row = lambda g: g[l].reshape(1, -1)
    w = {}
    for name, (gpre, wgu, wd, gpost) in {
            "ffn1": (g_ffn1_pre, w_ffn1_gu, w_ffn1_down, g_ffn1_post),
            "ffn2": (g_ffn2_pre, w_ffn2_gu, w_ffn2_down, g_ffn2_post)}.items():
        w[name] = (row(gpre), wgu[l].astype(BF), wd[l].astype(BF), row(gpost))
    cuts = np.cumsum([FOX_Q_W, FOX_KV_W, FOX_KV_W, FOX_HEADS, MLA_Q_LORA, MLA_KV_LORA]).tolist()
    wq, wk, wv, wf, wcq, wckv, wkr = jnp.split(w_in[l], cuts, axis=1)
    half = MLA_ROPE_DIM // 2
    pad = lambda a: jnp.pad(a, ((0, 0), (0, LANES - a.shape[1])))
    wkr_rot = jnp.concatenate([wkr[:, half:], wkr[:, :half]], axis=1)
    w["g_mix_pre"] = row(g_mix_pre)
    w["w_big"] = jnp.concatenate([wq, wk, wv, wcq, wckv], axis=1).astype(BF)
    w["w_small"] = jnp.concatenate([pad(wkr), pad(wkr_rot), pad(wf)], axis=1).astype(BF)
    w["b_f"] = pad(b_fgate[l].reshape(1, -1))
    w["g_q_norm"] = row(g_q_norm)
    w["g_kv_norm"] = row(g_kv_norm)
    wqu = w_q_up[l].reshape(MLA_Q_LORA, MLA_HEADS, MLA_NOPE_DIM + MLA_ROPE_DIM)
    w["w_q_nope"] = wqu[..., :MLA_NOPE_DIM].reshape(MLA_Q_LORA, -1).astype(BF)
    wqr = wqu[..., MLA_NOPE_DIM:]
    wqr_rot = jnp.concatenate([wqr[..., half:], wqr[..., :half]], axis=-1)
    w["w_q_rope_a"] = wqr.reshape(MLA_Q_LORA, -1).astype(BF)
    w["w_q_rope_b"] = wqr_rot.reshape(MLA_Q_LORA, -1).astype(BF)
    wkv = w_kv_up[l].reshape(MLA_KV_LORA, MLA_HEADS, MLA_NOPE_DIM + MLA_V_DIM)
    w["w_uk_t"] = jnp.transpose(wkv[..., :MLA_NOPE_DIM], (1, 2, 0)).astype(BF)
    w["w_uv"] = jnp.transpose(wkv[..., MLA_NOPE_DIM:], (1, 0, 2)).astype(BF)
    w["w_out"] = w_out[l].astype(BF)
    w["g_mix_post"] = row(g_mix_post)
    w["g_mem_tok"] = row(g_mem_tok)
    w["w_mem_kv"] = w_mem_kv[l].astype(BF)
    w["g_mem_pre"] = row(g_mem_pre)
    w["w_mem_q"] = w_mem_q[l].astype(BF)
    w["w_mem_o"] = w_mem_o[l].astype(BF)
    w["g_mem_post"] = row(g_mem_post)
    return w


def _rope_tables(pos):
    half = MLA_ROPE_DIM // 2
    inv_freq = ROPE_THETA ** (-jnp.arange(half, dtype=F32) / half)
    ang = pos.astype(F32)[:, None] * inv_freq[None, :]
    cos, sin = jnp.cos(ang), jnp.sin(ang)
    cos_t = jnp.tile(jnp.concatenate([cos, cos], axis=1), (1, MLA_HEADS))
    sin_t = jnp.tile(jnp.concatenate([-sin, sin], axis=1), (1, MLA_HEADS))
    return cos_t, sin_t


def kernel(x_prompt, x_sample, mem_prompt, cache_fox_k, cache_fox_v, cache_fox_logf, cache_mla_ckv, cache_mla_krope, cache_mem_k, cache_mem_v, page_table, g_ffn1_pre, w_ffn1_gu, w_ffn1_down, g_ffn1_post, g_mix_pre, w_in, b_fgate, g_q_norm, w_q_up, g_kv_norm, w_kv_up, w_out, g_mix_post, g_mem_tok, w_mem_kv, g_mem_pre, w_mem_q, w_mem_o, g_mem_post, g_ffn2_pre, w_ffn2_gu, w_ffn2_down, g_ffn2_post):
    n_seq, seq_len, _ = x_prompt.shape
    n_dec, dec_seq, _ = x_sample.shape
    depth = w_in.shape[0]
    n_pages = page_table.shape[1]
    past_len = n_pages * PAGE_SIZE
    n_mem = mem_prompt.shape[1]
    assert dec_seq == 4

    xp = x_prompt.reshape(n_seq * seq_len, D_MODEL)
    xs = x_sample.reshape(n_dec * dec_seq, D_MODEL)
    mem = mem_prompt.reshape(n_seq * n_mem, D_MODEL)
    cos_p, sin_p = _rope_tables(jnp.arange(seq_len, dtype=jnp.int32))
    pos_s = past_len + jnp.arange(dec_seq, dtype=jnp.int32)
    cos_s, sin_s = _rope_tables(jnp.tile(pos_s, n_dec))

    outs = {k: [] for k in ("pk", "pv", "plf", "pckv", "pkr", "pmk", "pmv",
                            "sk", "sv", "slf", "sckv", "skr")}
    for l in range(depth):
        w = _prep_weights(l, g_ffn1_pre, w_ffn1_gu, w_ffn1_down, g_ffn1_post, g_mix_pre,
                          w_in, b_fgate, g_q_norm, w_q_up, g_kv_norm, w_kv_up, w_out,
                          g_mix_post, g_mem_tok, w_mem_kv, g_mem_pre, w_mem_q, w_mem_o,
                          g_mem_post, g_ffn2_pre, w_ffn2_gu, w_ffn2_down, g_ffn2_post)

        xp = _ffn_half(xp, *w["ffn1"])
        pp = _mix_proj(xp, seq_len, w, cos_p, sin_p, prompt=True)
        o_fox = _fox_prompt(pp, n_seq, seq_len)
        o_lat = _mla_prompt(pp, n_seq, seq_len)
        xp, qm = _mix_out(xp, o_fox, o_lat, w)
        mk, mv = _memory_kv(mem, w["g_mem_tok"], w["w_mem_kv"])
        om = _mem_prompt(qm, mk.reshape(n_seq, n_mem, MEM_W), mv.reshape(n_seq, n_mem, MEM_W),
                         seq_len)
        xp = _memout_ffn(xp, om, w["w_mem_o"], w["g_mem_post"], *w["ffn2"])
        heads_last = lambda a: jnp.transpose(
            a.reshape(n_seq, FOX_KV_HEADS, FOX_HEAD_DIM, seq_len), (0, 3, 1, 2))
        outs["pk"].append(heads_last(pp["kT32"]))
        outs["pv"].append(heads_last(pp["vT32"]))
        outs["plf"].append(jnp.transpose(pp["lfT"], (0, 2, 1)))
        outs["pckv"].append(pp["ckv32"].reshape(n_seq, seq_len, MLA_KV_LORA))
        outs["pkr"].append(jnp.transpose(pp["krT32"], (0, 2, 1)))
        outs["pmk"].append(mk.reshape(n_seq, n_mem, MEM_HEADS, MEM_HEAD_DIM))
        outs["pmv"].append(mv.reshape(n_seq, n_mem, MEM_HEADS, MEM_HEAD_DIM))

        xs = _ffn_half(xs, *w["ffn1"])
        ps = _mix_proj(xs, n_dec * dec_seq, w, cos_s, sin_s, prompt=False)
        n_pool = cache_fox_k.shape[1]
        kc = jnp.transpose(cache_fox_k[l], (0, 2, 3, 1)).reshape(n_pool, FOX_KV_W, PAGE_SIZE)
        vc = jnp.transpose(cache_fox_v[l], (0, 2, 3, 1)).reshape(n_pool, FOX_KV_W, PAGE_SIZE)
        krc = jnp.transpose(cache_mla_krope[l], (0, 2, 1))
        lfc = jnp.transpose(cache_fox_logf[l], (0, 2, 1))
        o_fox, o_lat = _decode_attention(ps, page_table, kc, vc, cache_mla_ckv[l], krc, lfc)
        xs, qm = _mix_out(xs, o_fox, o_lat, w)
        om = _mem_decode(qm, cache_mem_k[l], cache_mem_v[l])
        xs = _memout_ffn(xs, om, w["w_mem_o"], w["g_mem_post"], *w["ffn2"])
        outs["sk"].append(ps["k32"].reshape(n_dec, dec_seq, FOX_KV_HEADS, FOX_HEAD_DIM))
        outs["sv"].append(ps["v32"].reshape(n_dec, dec_seq, FOX_KV_HEADS, FOX_HEAD_DIM))
        outs["slf"].append(ps["lf"].reshape(n_dec, dec_seq, FOX_HEADS))
        outs["sckv"].append(ps["ckv32"].reshape(n_dec, dec_seq, MLA_KV_LORA))
        outs["skr"].append(ps["kr32"].reshape(n_dec, dec_seq, MLA_ROPE_DIM))

    st = {k: jnp.stack(v) for k, v in outs.items()}
    return (xp.reshape(n_seq, seq_len, D_MODEL), xs.reshape(n_dec, dec_seq, D_MODEL),
            st["pk"], st["pv"], st["plf"], st["pckv"], st["pkr"], st["pmk"], st["pmv"],
            st["sk"], st["sv"], st["slf"], st["sckv"], st["skr"])
```

```python
import functools

import numpy as np
import jax
import jax.numpy as jnp
from jax import lax
from jax.experimental import pallas as pl
from jax.experimental.pallas import tpu as pltpu

BF = jnp.bfloat16
F32 = jnp.float32

D_MODEL = 1024
PAGE_SIZE = 128
FOX_HEADS = 8
FOX_KV_HEADS = 4
FOX_GROUP = FOX_HEADS // FOX_KV_HEADS
FOX_HEAD_DIM = 64
FOX_Q_W = FOX_HEADS * FOX_HEAD_DIM
FOX_KV_W = FOX_KV_HEADS * FOX_HEAD_DIM
MLA_HEADS = 4
MLA_Q_LORA = 256
MLA_KV_LORA = 256
MLA_NOPE_DIM = 128
MLA_ROPE_DIM = 64
MLA_V_DIM = 128
ROPE_THETA = 10000.0
MEM_HEADS = 4
MEM_HEAD_DIM = 128
MEM_W = MEM_HEADS * MEM_HEAD_DIM
D_FF = 2816
RMS_EPS = 1e-6
NEG_INF = -1e30

LANES = 128
VMEM_LIMIT_BYTES = 56 * 1024 * 1024

LOG2E = 1.4426950408889634
FOX_SCALE = FOX_HEAD_DIM ** -0.5 * LOG2E
MLA_SCALE = (MLA_NOPE_DIM + MLA_ROPE_DIM) ** -0.5 * LOG2E
MEM_SCALE = MEM_HEAD_DIM ** -0.5

ROW_TILE = 512
FF_CHUNK = 256
ATT_TQ = 512
ATT_TK = 1024
DEC_PAGES = 32
MEM_DEC_BATCH = 8


def _params(*sem):
    return pltpu.CompilerParams(dimension_semantics=sem,
                                vmem_limit_bytes=VMEM_LIMIT_BYTES)


def _rms(x, g):
    return x * lax.rsqrt(jnp.mean(x * x, axis=-1, keepdims=True) + RMS_EPS) * g


def _dot(a, b):
    return jnp.dot(a, b, preferred_element_type=F32)


def _dot_nt(a, b):
    return lax.dot_general(a, b, (((1,), (1,)), ((), ())), preferred_element_type=F32)


def _const_spec(shape):
    zeros = (0,) * len(shape)
    return pl.BlockSpec(shape, lambda *_: zeros)


def _row_tile(n):
    return ROW_TILE if n % ROW_TILE == 0 else n


def _swiglu_half(x, gpre, wgu_ref, wd_ref, gpost):
    h = _rms(x, gpre).astype(BF)
    acc = jnp.zeros(x.shape, F32)
    for c in range(D_FF // FF_CHUNK):
        g = _dot(h, wgu_ref[:, c * FF_CHUNK:(c + 1) * FF_CHUNK])
        u = _dot(h, wgu_ref[:, D_FF + c * FF_CHUNK:D_FF + (c + 1) * FF_CHUNK])
        a = (g * jax.nn.sigmoid(g) * u).astype(BF)
        acc = acc + _dot(a, wd_ref[c * FF_CHUNK:(c + 1) * FF_CHUNK, :])
    return x + 0.5 * _rms(acc, gpost)


def _ffn_kernel(x_ref, gpre_ref, wgu_ref, wd_ref, gpost_ref, o_ref):
    o_ref[...] = _swiglu_half(x_ref[...], gpre_ref[...], wgu_ref, wd_ref, gpost_ref[...])


def _ffn_half(x, gpre, wgu, wd, gpost):
    n = x.shape[0]
    tm = _row_tile(n)
    row = pl.BlockSpec((tm, D_MODEL), lambda i: (i, 0))
    return pl.pallas_call(
        _ffn_kernel,
        grid=(n // tm,),
        in_specs=[row, _const_spec((1, D_MODEL)), _const_spec((D_MODEL, 2 * D_FF)),
                  _const_spec((D_FF, D_MODEL)), _const_spec((1, D_MODEL))],
        out_specs=row,
        out_shape=jax.ShapeDtypeStruct((n, D_MODEL), F32),
        compiler_params=_params("parallel"),
        name="ffn_half",
    )(x, gpre, wgu, wd, gpost)


def _memout_ffn_kernel(x_ref, om_ref, wmo_ref, gmpost_ref, gpre_ref, wgu_ref, wd_ref, gpost_ref,
                       o_ref):
    x = x_ref[...] + _rms(_dot(om_ref[...], wmo_ref[...]), gmpost_ref[...])
    o_ref[...] = _swiglu_half(x, gpre_ref[...], wgu_ref, wd_ref, gpost_ref[...])


def _memout_ffn(x, om, wmo, gmpost, gpre, wgu, wd, gpost):
    n = x.shape[0]
    tm = _row_tile(n)
    row = pl.BlockSpec((tm, D_MODEL), lambda i: (i, 0))
    return pl.pallas_call(
        _memout_ffn_kernel,
        grid=(n // tm,),
        in_specs=[row, pl.BlockSpec((tm, MEM_W), lambda i: (i, 0)),
                  _const_spec((MEM_W, D_MODEL)), _const_spec((1, D_MODEL)),
                  _const_spec((1, D_MODEL)), _const_spec((D_MODEL, 2 * D_FF)),
                  _const_spec((D_FF, D_MODEL)), _const_spec((1, D_MODEL))],
        out_specs=row,
        out_shape=jax.ShapeDtypeStruct((n, D_MODEL), F32),
        compiler_params=_params("parallel"),
        name="memout_ffn",
    )(x, om, wmo, gmpost, gpre, wgu, wd, gpost)


BIG_W = FOX_Q_W + 2 * FOX_KV_W + MLA_Q_LORA + MLA_KV_LORA
SMALL_W = 3 * LANES


def _log_sigmoid(x):
    return jnp.minimum(x, 0.0) - jnp.log1p(jnp.exp(-jnp.abs(x)))


MIX_COMMON = ("qlat", "qrope", "ckv32")
MIX_PROMPT = ("kT32", "vT32", "krT32", "lfT", "qT", "k16", "vT1", "ckv16", "ckvT", "krT", "c", "cT")
MIX_DECODE = ("q", "k32", "v32", "kr32", "lf")
FOX_VT_ROWS = FOX_HEAD_DIM + 16


def _mixproj_kernel(x_ref, g_ref, wbig_ref, wsmall_ref, bf_ref, gq_ref, gkv_ref,
                    wqn_ref, wqra_ref, wqrb_ref, wuk_ref, cos_ref, sin_ref, *rest,
                    names, tiles_per_seq):
    o = dict(zip(names, rest))
    carry_ref = rest[-1]
    prompt = "qT" in o
    i = pl.program_id(0)
    tm = x_ref.shape[0]
    h = _rms(x_ref[...], g_ref[...]).astype(BF)
    big = _dot(h, wbig_ref[...])
    small = _dot(h, wsmall_ref[...])
    cos = cos_ref[...]
    sin = sin_ref[...]

    q = big[:, :FOX_Q_W] * FOX_SCALE
    k = big[:, FOX_Q_W:FOX_Q_W + FOX_KV_W]
    v = big[:, FOX_Q_W + FOX_KV_W:FOX_Q_W + 2 * FOX_KV_W]
    lf = _log_sigmoid(small[:, 2 * LANES:] + bf_ref[...])
    if prompt:
        o["kT32"][0] = k.T
        o["k16"][...] = k.astype(BF)
        vT = v.T
        o["vT32"][0] = vT
        ones_row = (lax.broadcasted_iota(jnp.int32, (FOX_VT_ROWS - FOX_HEAD_DIM, tm), 0) == 0).astype(BF)
        for kvh in range(FOX_KV_HEADS):
            o["vT1"][0, kvh * FOX_VT_ROWS:kvh * FOX_VT_ROWS + FOX_HEAD_DIM] = (
                vT[kvh * FOX_HEAD_DIM:(kvh + 1) * FOX_HEAD_DIM].astype(BF))
            o["vT1"][0, kvh * FOX_VT_ROWS + FOX_HEAD_DIM:(kvh + 1) * FOX_VT_ROWS] = ones_row
        qT = q.T.astype(BF)
        zero = jnp.zeros((FOX_HEAD_DIM, tm), BF)
        for hh in range(FOX_HEADS):
            piece = qT[hh * FOX_HEAD_DIM:(hh + 1) * FOX_HEAD_DIM]
            odd = (hh // FOX_GROUP) % 2 == 1
            o["qT"][0, hh * LANES:(hh + 1) * LANES] = jnp.concatenate(
                [zero, piece] if odd else [piece, zero], axis=0)

        lfT = lf.T
        o["lfT"][0] = lfT[:FOX_HEADS]

        @pl.when(i % tiles_per_seq == 0)
        def _():
            carry_ref[...] = jnp.zeros_like(carry_ref)

        lane = lax.broadcasted_iota(jnp.int32, lfT.shape, 1)
        run = lfT
        sh = 1
        while sh < tm:
            run = run + jnp.where(lane >= sh, pltpu.roll(run, sh, axis=1), 0.0)
            sh *= 2
        run = run + carry_ref[...]
        carry_ref[...] = run[:, tm - 1:tm]
        run2 = run * LOG2E
        o["cT"][0] = run2[:FOX_HEADS]
        o["c"][...] = run2.T[:, :FOX_HEADS]
    else:
        o["q"][...] = q.astype(BF)
        o["k32"][...] = k
        o["v32"][...] = v
        o["lf"][...] = lf[:, :FOX_HEADS]

    cq = big[:, FOX_Q_W + 2 * FOX_KV_W:FOX_Q_W + 2 * FOX_KV_W + MLA_Q_LORA]
    cqn = _rms(cq, gq_ref[...]).astype(BF)
    qn = _dot(cqn, wqn_ref[...]).astype(BF)
    for hd in range(MLA_HEADS):
        ql = _dot(qn[:, hd * MLA_NOPE_DIM:(hd + 1) * MLA_NOPE_DIM], wuk_ref[hd])
        o["qlat"][:, hd * MLA_KV_LORA:(hd + 1) * MLA_KV_LORA] = (ql * MLA_SCALE).astype(BF)
    qr = _dot(cqn, wqra_ref[...]) * cos + _dot(cqn, wqrb_ref[...]) * sin
    o["qrope"][...] = (qr * MLA_SCALE).astype(BF)

    ckv = _rms(big[:, BIG_W - MLA_KV_LORA:], gkv_ref[...])
    o["ckv32"][...] = ckv
    kr = small[:, :LANES] * cos[:, :LANES] + small[:, LANES:2 * LANES] * sin[:, :LANES]
    if prompt:
        o["ckv16"][...] = ckv.astype(BF)
        o["ckvT"][0] = ckv.T.astype(BF)
        krT = kr.T[:MLA_ROPE_DIM]
        o["krT32"][0] = krT
        o["krT"][0] = krT.astype(BF)
    else:
        o["kr32"][...] = kr[:, :MLA_ROPE_DIM]


def _mix_proj(x, seq_len, w, cos_tab, sin_tab, prompt):
    n = x.shape[0]
    tm = _row_tile(min(n, seq_len))
    n_seq = n // seq_len
    tps = seq_len // tm
    tab_tiles = cos_tab.shape[0] // tm
    row = lambda width: pl.BlockSpec((tm, width), lambda i: (i, 0))
    colT = lambda height: pl.BlockSpec((1, height, tm), lambda i: (i // tps, 0, i % tps))
    tab = pl.BlockSpec((tm, 2 * LANES), lambda i: (i % tab_tiles, 0))
    rows = lambda width, dt: (jax.ShapeDtypeStruct((n, width), dt), row(width))
    cols = lambda height, dt: (jax.ShapeDtypeStruct((n_seq, height, seq_len), dt), colT(height))
    outputs = {
        "q": rows(FOX_Q_W, BF), "qlat": rows(MLA_HEADS * MLA_KV_LORA, BF),
        "qrope": rows(MLA_HEADS * MLA_ROPE_DIM, BF), "ckv32": rows(MLA_KV_LORA, F32),
        "kT32": cols(FOX_KV_W, F32), "vT32": cols(FOX_KV_W, F32),
        "krT32": cols(MLA_ROPE_DIM, F32), "lfT": cols(FOX_HEADS, F32),
        "qT": cols(FOX_HEADS * LANES, BF), "k16": rows(FOX_KV_W, BF),
        "vT1": cols(FOX_KV_HEADS * FOX_VT_ROWS, BF),
        "ckv16": rows(MLA_KV_LORA, BF), "ckvT": cols(MLA_KV_LORA, BF),
        "krT": cols(MLA_ROPE_DIM, BF), "c": rows(FOX_HEADS, F32), "cT": cols(FOX_HEADS, F32),
        "k32": rows(FOX_KV_W, F32), "v32": rows(FOX_KV_W, F32),
        "kr32": rows(MLA_ROPE_DIM, F32), "lf": rows(FOX_HEADS, F32),
    }
    names = MIX_COMMON + (MIX_PROMPT if prompt else MIX_DECODE)
    out_shape = tuple(outputs[k][0] for k in names)
    out_specs = tuple(outputs[k][1] for k in names)
    in_specs = [
        row(D_MODEL), _const_spec((1, D_MODEL)), _const_spec((D_MODEL, BIG_W)),
        _const_spec((D_MODEL, SMALL_W)), _const_spec((1, LANES)),
        _const_spec((1, MLA_Q_LORA)), _const_spec((1, MLA_KV_LORA)),
        _const_spec((MLA_Q_LORA, MLA_HEADS * MLA_NOPE_DIM)),
        _const_spec((MLA_Q_LORA, MLA_HEADS * MLA_ROPE_DIM)),
        _const_spec((MLA_Q_LORA, MLA_HEADS * MLA_ROPE_DIM)),
        _const_spec((MLA_HEADS, MLA_NOPE_DIM, MLA_KV_LORA)), tab, tab,
    ]
    outs = pl.pallas_call(
        functools.partial(_mixproj_kernel, names=names, tiles_per_seq=tps),
        grid=(n // tm,),
        in_specs=in_specs,
        out_specs=out_specs,
        out_shape=out_shape,
        scratch_shapes=[pltpu.VMEM((LANES, 1), F32)],
        compiler_params=_params("arbitrary"),
        name="mix_proj",
    )(x, w["g_mix_pre"], w["w_big"], w["w_small"], w["b_f"], w["g_q_norm"],
      w["g_kv_norm"], w["w_q_nope"], w["w_q_rope_a"], w["w_q_rope_b"], w["w_uk_t"],
      cos_tab, sin_tab)
    return dict(zip(names, outs))


def _online_update(m_ref, l_ref, acc_ref, h, z, v, row_bias=None):
    m_prev = m_ref[h]
    m_z = jnp.max(z, axis=1, keepdims=True)
    if row_bias is None:
        m_new = jnp.maximum(m_prev, m_z)
        shift = m_new
    else:
        m_new = jnp.maximum(m_prev, m_z + row_bias)
        shift = m_new - row_bias
    alpha = jnp.exp2(m_prev - m_new)
    p = jnp.exp2(z - shift)
    if l_ref is not None:
        l_ref[h] = alpha * l_ref[h] + jnp.sum(p, axis=1, keepdims=True)
    acc_ref[h] = alpha * acc_ref[h] + _dot(p.astype(BF), v)
    m_ref[h] = m_new


def _init_state(m_ref, l_ref, acc_ref):
    m_ref[...] = jnp.full_like(m_ref, NEG_INF)
    if l_ref is not None:
        l_ref[...] = jnp.zeros_like(l_ref)
    acc_ref[...] = jnp.zeros_like(acc_ref)


def _causal_keep(shape, q0, k0, q_axis):
    qpos = q0 + lax.broadcasted_iota(jnp.int32, shape, q_axis)
    kpos = k0 + lax.broadcasted_iota(jnp.int32, shape, 1 - q_axis)
    return qpos >= kpos


def _sweep_key_blocks(block, i, tq, q_axis=0):
    assert ATT_TK == 2 * tq
    q0 = i * tq
    n_full = q0 // ATT_TK

    def body(kj, carry):
        block(pl.multiple_of(kj * ATT_TK, ATT_TK), ATT_TK, None)
        return carry

    lax.fori_loop(0, n_full, body, 0)
    k0 = pl.multiple_of(n_full * ATT_TK, ATT_TK)

    def last(width):
        shape = (tq, width) if q_axis == 0 else (width, tq)
        block(k0, width, _causal_keep(shape, q0, k0, q_axis))

    pl.when(q0 == k0)(lambda: last(tq))
    pl.when(q0 != k0)(lambda: last(ATT_TK))


def _fox_prompt_kernel(qT_ref, cq_ref, k_ref, vT1_ref, ck_ref, o_ref, m_ref, acc_ref):
    i = pl.program_id(1)
    tq = qT_ref.shape[2]
    _init_state(m_ref, None, acc_ref)

    def block(k0, width, keep):
        keys = pl.ds(k0, width)

        def scores(hh):
            pair = hh // (2 * FOX_GROUP)
            kq = _dot(k_ref[keys, pair * LANES:(pair + 1) * LANES], qT_ref[0, hh * LANES:(hh + 1) * LANES, :])
            return kq - ck_ref[keys, hh:hh + 1]

        z = scores(0)
        for hh in range(FOX_HEADS):
            z_next = scores(hh + 1) if hh + 1 < FOX_HEADS else None
            if keep is not None:
                z = jnp.where(keep, z, NEG_INF)
            kvh = hh // FOX_GROUP
            cq = cq_ref[0, hh:hh + 1, :]
            m_prev = m_ref[hh]
            m_new = jnp.maximum(m_prev, jnp.max(z, axis=0, keepdims=True) + cq)
            alpha = jnp.exp2(m_prev - m_new)
            p = jnp.exp2(z - (m_new - cq))
            vT1 = vT1_ref[0, kvh * FOX_VT_ROWS:(kvh + 1) * FOX_VT_ROWS, keys]
            acc_ref[hh] = alpha * acc_ref[hh] + _dot(vT1, p.astype(BF))
            m_ref[hh] = m_new
            z = z_next

    _sweep_key_blocks(block, i, tq, q_axis=1)
    for hh in range(FOX_HEADS):
        acc = acc_ref[hh]
        oT = acc[:FOX_HEAD_DIM] / acc[FOX_HEAD_DIM:FOX_HEAD_DIM + 1]
        o_ref[:, hh * FOX_HEAD_DIM:(hh + 1) * FOX_HEAD_DIM] = oT.T.astype(BF)


def _fox_prompt(p, n_seq, seq_len):
    tq = ATT_TQ
    nq = seq_len // tq
    return pl.pallas_call(
        _fox_prompt_kernel,
        grid=(n_seq, nq),
        in_specs=[
            pl.BlockSpec((1, FOX_HEADS * LANES, tq), lambda b, i: (b, 0, i)),
            pl.BlockSpec((1, FOX_HEADS, tq), lambda b, i: (b, 0, i)),
            pl.BlockSpec((seq_len, FOX_KV_W), lambda b, i: (b, 0)),
            pl.BlockSpec((1, FOX_KV_HEADS * FOX_VT_ROWS, seq_len), lambda b, i: (b, 0, 0)),
            pl.BlockSpec((seq_len, FOX_HEADS), lambda b, i: (b, 0)),
        ],
        out_specs=pl.BlockSpec((tq, FOX_Q_W), lambda b, i: (b * nq + i, 0)),
        out_shape=jax.ShapeDtypeStruct((n_seq * seq_len, FOX_Q_W), BF),
        scratch_shapes=[pltpu.VMEM((FOX_HEADS, 1, tq), F32),
                        pltpu.VMEM((FOX_HEADS, FOX_VT_ROWS, tq), F32)],
        compiler_params=_params("parallel", "arbitrary"),
        name="fox_prompt",
    )(p["qT"], p["cT"], p["k16"], p["vT1"], p["c"])


def _mla_prompt_kernel(ql_ref, qr_ref, ckvT_ref, krT_ref, ckv_ref, o_ref, qrs_ref, m_ref, l_ref,
                       acc_ref):
    i = pl.program_id(1)
    tq = ql_ref.shape[0]
    for hd in range(MLA_HEADS):
        qrs_ref[hd] = qr_ref[:, hd * MLA_ROPE_DIM:(hd + 1) * MLA_ROPE_DIM]
    _init_state(m_ref, l_ref, acc_ref)

    def block(k0, width, keep):
        keys = pl.ds(k0, width)
        ckvT = ckvT_ref[0, :, keys]
        krT = krT_ref[0, :, keys]
        ckv = ckv_ref[keys, :]

        def scores(hd):
            return (_dot(ql_ref[:, hd * MLA_KV_LORA:(hd + 1) * MLA_KV_LORA], ckvT)
                    + _dot(qrs_ref[hd], krT))

        z = scores(0)
        for hd in range(MLA_HEADS):
            z_next = scores(hd + 1) if hd + 1 < MLA_HEADS else None
            if keep is not None:
                z = jnp.where(keep, z, NEG_INF)
            _online_update(m_ref, l_ref, acc_ref, hd, z, ckv)
            z = z_next

    _sweep_key_blocks(block, i, tq)
    for hd in range(MLA_HEADS):
        o_ref[:, hd * MLA_KV_LORA:(hd + 1) * MLA_KV_LORA] = (acc_ref[hd] / l_ref[hd]).astype(BF)


def _mla_prompt(p, n_seq, seq_len):
    tq = ATT_TQ
    nq = seq_len // tq
    lat_w = MLA_HEADS * MLA_KV_LORA
    return pl.pallas_call(
        _mla_prompt_kernel,
        grid=(n_seq, nq),
        in_specs=[
            pl.BlockSpec((tq, lat_w), lambda b, i: (b * nq + i, 0)),
            pl.BlockSpec((tq, MLA_HEADS * MLA_ROPE_DIM), lambda b, i: (b * nq + i, 0)),
            pl.BlockSpec((1, MLA_KV_LORA, seq_len), lambda b, i: (b, 0, 0)),
            pl.BlockSpec((1, MLA_ROPE_DIM, seq_len), lambda b, i: (b, 0, 0)),
            pl.BlockSpec((seq_len, MLA_KV_LORA), lambda b, i: (b, 0)),
        ],
        out_specs=pl.BlockSpec((tq, lat_w), lambda b, i: (b * nq + i, 0)),
        out_shape=jax.ShapeDtypeStruct((n_seq * seq_len, lat_w), BF),
        scratch_shapes=[pltpu.VMEM((MLA_HEADS, tq, MLA_ROPE_DIM), BF),
                        pltpu.VMEM((MLA_HEADS, tq, 1), F32),
                        pltpu.VMEM((MLA_HEADS, tq, 1), F32),
                        pltpu.VMEM((MLA_HEADS, tq, MLA_KV_LORA), F32)],
        compiler_params=_params("parallel", "arbitrary"),
        name="mla_prompt",
    )(p["qlat"], p["qrope"], p["ckvT"], p["krT"], p["ckv16"])


DEC_FOX_ROWS = 4 * FOX_HEADS
DEC_MLA_ROWS = 4 * MLA_HEADS


def _decode_kernel(pt_ref, qbd_ref, lfnew_ref, knew_ref, vnew_ref, qlat_ref, qrope_ref,
                   ckvnew_ref, krnew_ref, kc_ref, vc_ref, ckvc_ref, krc_ref, lfc_ref,
                   of_ref, om_ref,
                   kbuf, vbuf, ckvbuf, krbuf, lfbuf, sems,
                   mf_ref, lf_ref, accf_ref, mm_ref, lm_ref, accm_ref, rcarry_ref,
                   *, n_chunks, n_steps):
    t = pl.program_id(0)
    n_keys = DEC_PAGES * PAGE_SIZE

    def chunk_copies(step, slot):
        b = step // n_chunks
        first = (n_chunks - 1 - step % n_chunks) * DEC_PAGES
        copies = []
        for j in range(DEC_PAGES):
            pid = pt_ref[b, first + j]
            lanes = pl.ds(j * PAGE_SIZE, PAGE_SIZE)
            copies += [
                pltpu.make_async_copy(kc_ref.at[pid], kbuf.at[slot, :, lanes], sems.at[slot, 0]),
                pltpu.make_async_copy(vc_ref.at[pid], vbuf.at[slot, :, lanes], sems.at[slot, 1]),
                pltpu.make_async_copy(ckvc_ref.at[pid], ckvbuf.at[slot, lanes, :], sems.at[slot, 2]),
                pltpu.make_async_copy(krc_ref.at[pid], krbuf.at[slot, :, lanes], sems.at[slot, 3]),
                pltpu.make_async_copy(lfc_ref.at[pid], lfbuf.at[slot, :, lanes], sems.at[slot, 4]),
            ]
        return copies

    slot = t % 2

    @pl.when(t == 0)
    def _():
        for cp in chunk_copies(t, slot):
            cp.start()

    @pl.when(t + 1 < n_steps)
    def _():
        for cp in chunk_copies(t + 1, 1 - slot):
            cp.start()

    @pl.when(t % n_chunks == 0)
    def _():
        mf_ref[...] = jnp.full_like(mf_ref, NEG_INF)
        lf_ref[...] = jnp.zeros_like(lf_ref)
        accf_ref[...] = jnp.zeros_like(accf_ref)
        mm_ref[...] = jnp.full_like(mm_ref, NEG_INF)
        lm_ref[...] = jnp.zeros_like(lm_ref)
        accm_ref[...] = jnp.zeros_like(accm_ref)
        rcarry_ref[...] = jnp.zeros_like(rcarry_ref)

    for cp in chunk_copies(t, slot):
        cp.wait()

    lfnew = lfnew_ref[0]
    parts = [lfnew[0:FOX_HEADS]]
    for u in range(1, 4):
        parts.append(parts[-1] + lfnew[u * FOX_HEADS:(u + 1) * FOX_HEADS])
    ncol = jnp.concatenate(parts, axis=0) * LOG2E
    qbd = qbd_ref[0]

    lfp = lfbuf[slot]
    lane = lax.broadcasted_iota(jnp.int32, lfp.shape, 1)
    run = lfp
    sh = 1
    while sh < n_keys:
        run = run + jnp.where(lane < n_keys - sh, pltpu.roll(run, n_keys - sh, axis=1), 0.0)
        sh *= 2
    later = run - lfp + rcarry_ref[...]
    rcarry_ref[...] = rcarry_ref[...] + run[:, 0:1]
    bias = jnp.concatenate([later * LOG2E] * 4, axis=0) + ncol

    def update(m_ref, l_ref, acc_ref, s, pv_fn):
        m_prev = m_ref[...]
        m_new = jnp.maximum(m_prev, jnp.max(s, axis=1, keepdims=True))
        alpha = jnp.exp2(m_prev - m_new)
        p = jnp.exp2(s - m_new)
        l_ref[...] = alpha * l_ref[...] + jnp.sum(p, axis=1, keepdims=True)
        acc_ref[...] = alpha * acc_ref[...] + pv_fn(p)
        m_ref[...] = m_new

    s_f = _dot(qbd, kbuf[slot].astype(BF)) + bias
    ckv = ckvbuf[slot].astype(BF)
    qlat = qlat_ref[0]
    qrope = qrope_ref[0]
    s_m = _dot_nt(qlat, ckv) + _dot(qrope, krbuf[slot].astype(BF))
    update(mf_ref, lf_ref, accf_ref, s_f,
           lambda p: _dot_nt(p.astype(BF), vbuf[slot].astype(BF)))
    update(mm_ref, lm_ref, accm_ref, s_m, lambda p: _dot(p.astype(BF), ckv))

    @pl.when(t % n_chunks == n_chunks - 1)
    def _():
        qf = qbd.astype(F32)
        rowf = lax.broadcasted_iota(jnp.int32, (DEC_FOX_ROWS, 1), 0)
        knew = knew_ref[0]
        vnew = vnew_ref[0]
        for u in range(4):
            s = jnp.sum(qf * knew[u:u + 1, :], axis=1, keepdims=True)
            n_u = jnp.concatenate([ncol[u * FOX_HEADS:(u + 1) * FOX_HEADS]] * 4, axis=0)
            s = jnp.where(rowf >= u * FOX_HEADS, s + (ncol - n_u), NEG_INF)
            update(mf_ref, lf_ref, accf_ref, s, lambda p: p * vnew[u:u + 1, :])
        o = accf_ref[...] / lf_ref[...]
        row = lax.broadcasted_iota(jnp.int32, o.shape, 0)
        col = lax.broadcasted_iota(jnp.int32, o.shape, 1)
        o = jnp.where(col // FOX_HEAD_DIM == (row % FOX_HEADS) // FOX_GROUP, o, 0.0)
        of_ref[0] = (o[:, 0:64] + o[:, 64:128]) + (o[:, 128:192] + o[:, 192:256])

        qlf = qlat.astype(F32)
        qrf = qrope.astype(F32)
        rowm = lax.broadcasted_iota(jnp.int32, (DEC_MLA_ROWS, 1), 0)
        ckvnew = ckvnew_ref[0]
        krnew = krnew_ref[0]
        for u in range(4):
            s = (jnp.sum(qlf * ckvnew[u:u + 1, :], axis=1, keepdims=True)
                 + jnp.sum(qrf * krnew[u:u + 1, :], axis=1, keepdims=True))
            s = jnp.where(rowm >= u * MLA_HEADS, s, NEG_INF)
            update(mm_ref, lm_ref, accm_ref, s, lambda p: p * ckvnew[u:u + 1, :])
        om_ref[0] = accm_ref[...] / lm_ref[...]


def _decode_attention(ps, page_table, kc, vc, ckvc, krc, lfc):
    n_b, n_pages = page_table.shape
    n_chunks = n_pages // DEC_PAGES
    n_steps = n_b * n_chunks
    n_keys = DEC_PAGES * PAGE_SIZE

    q = ps["q"].reshape(n_b, 4, FOX_KV_HEADS, FOX_GROUP, FOX_HEAD_DIM)
    eye = jnp.eye(FOX_KV_HEADS, dtype=BF)
    qbd = jnp.einsum("btkgd,kj->btkgjd", q, eye).reshape(n_b, DEC_FOX_ROWS, FOX_KV_W)
    lfnew = ps["lf"].reshape(n_b, DEC_FOX_ROWS, 1)
    knew = ps["k32"].reshape(n_b, 4, FOX_KV_W)
    vnew = ps["v32"].reshape(n_b, 4, FOX_KV_W)
    qlat = ps["qlat"].reshape(n_b, DEC_MLA_ROWS, MLA_KV_LORA)
    qrope = ps["qrope"].reshape(n_b, DEC_MLA_ROWS, MLA_ROPE_DIM)
    ckvnew = ps["ckv32"].reshape(n_b, 4, MLA_KV_LORA)
    krnew = ps["kr32"].reshape(n_b, 4, MLA_ROPE_DIM)

    per_b = lambda r, w: pl.BlockSpec((1, r, w), lambda t, pt: (t // n_chunks, 0, 0))
    hbm = pl.BlockSpec(memory_space=pl.ANY)
    grid_spec = pltpu.PrefetchScalarGridSpec(
        num_scalar_prefetch=1,
        grid=(n_steps,),
        in_specs=[per_b(DEC_FOX_ROWS, FOX_KV_W), per_b(DEC_FOX_ROWS, 1),
                  per_b(4, FOX_KV_W), per_b(4, FOX_KV_W),
                  per_b(DEC_MLA_ROWS, MLA_KV_LORA), per_b(DEC_MLA_ROWS, MLA_ROPE_DIM),
                  per_b(4, MLA_KV_LORA), per_b(4, MLA_ROPE_DIM),
                  hbm, hbm, hbm, hbm, hbm],
        out_specs=[per_b(DEC_FOX_ROWS, FOX_HEAD_DIM), per_b(DEC_MLA_ROWS, MLA_KV_LORA)],
        scratch_shapes=[
            pltpu.VMEM((2, FOX_KV_W, n_keys), F32),
            pltpu.VMEM((2, FOX_KV_W, n_keys), F32),
            pltpu.VMEM((2, n_keys, MLA_KV_LORA), F32),
            pltpu.VMEM((2, MLA_ROPE_DIM, n_keys), F32),
            pltpu.VMEM((2, FOX_HEADS, n_keys), F32),
            pltpu.SemaphoreType.DMA((2, 5)),
            pltpu.VMEM((DEC_FOX_ROWS, 1), F32), pltpu.VMEM((DEC_FOX_ROWS, 1), F32),
            pltpu.VMEM((DEC_FOX_ROWS, FOX_KV_W), F32),
            pltpu.VMEM((DEC_MLA_ROWS, 1), F32), pltpu.VMEM((DEC_MLA_ROWS, 1), F32),
            pltpu.VMEM((DEC_MLA_ROWS, MLA_KV_LORA), F32),
            pltpu.VMEM((FOX_HEADS, 1), F32),
        ],
    )
    o_fox, o_lat = pl.pallas_call(
        functools.partial(_decode_kernel, n_chunks=n_chunks, n_steps=n_steps),
        grid_spec=grid_spec,
        out_shape=(jax.ShapeDtypeStruct((n_b, DEC_FOX_ROWS, FOX_HEAD_DIM), F32),
                   jax.ShapeDtypeStruct((n_b, DEC_MLA_ROWS, MLA_KV_LORA), F32)),
        compiler_params=_params("arbitrary"),
        name="decode_attention",
    )(page_table, qbd, lfnew, knew, vnew, qlat, qrope, ckvnew, krnew, kc, vc, ckvc, krc, lfc)
    return (o_fox.reshape(n_b * 4, FOX_Q_W).astype(BF),
            o_lat.reshape(n_b * 4, MLA_HEADS * MLA_KV_LORA).astype(BF))


def _mixout_kernel(x_ref, of_ref, ol_ref, wuv_ref, wo_ref, gpost_ref, gmpre_ref, wmq_ref,
                   x_out_ref, qm_ref):
    parts = [of_ref[...]]
    for hd in range(MLA_HEADS):
        om = _dot(ol_ref[:, hd * MLA_KV_LORA:(hd + 1) * MLA_KV_LORA], wuv_ref[hd])
        parts.append(om.astype(BF))
    o = jnp.concatenate(parts, axis=1)
    x = x_ref[...] + _rms(_dot(o, wo_ref[...]), gpost_ref[...])
    x_out_ref[...] = x
    hm = _rms(x, gmpre_ref[...]).astype(BF)
    qm_ref[...] = (_dot(hm, wmq_ref[...]) * MEM_SCALE).astype(BF)


def _mix_out(x, o_fox, o_lat, w):
    n = x.shape[0]
    tm = _row_tile(n)
    row = lambda width: pl.BlockSpec((tm, width), lambda i: (i, 0))
    mix_w = FOX_Q_W + MLA_HEADS * MLA_V_DIM
    return pl.pallas_call(
        _mixout_kernel,
        grid=(n // tm,),
        in_specs=[row(D_MODEL), row(FOX_Q_W), row(MLA_HEADS * MLA_KV_LORA),
                  _const_spec((MLA_HEADS, MLA_KV_LORA, MLA_V_DIM)),
                  _const_spec((mix_w, D_MODEL)), _const_spec((1, D_MODEL)),
                  _const_spec((1, D_MODEL)), _const_spec((D_MODEL, MEM_W))],
        out_specs=(row(D_MODEL), row(MEM_W)),
        out_shape=(jax.ShapeDtypeStruct((n, D_MODEL), F32),
                   jax.ShapeDtypeStruct((n, MEM_W), BF)),
        compiler_params=_params("parallel"),
        name="mix_out",
    )(x, o_fox, o_lat, w["w_uv"], w["w_out"], w["g_mix_post"], w["g_mem_pre"], w["w_mem_q"])


def _memkv_kernel(mem_ref, g_ref, w_ref, k_ref, v_ref):
    kv = _dot(_rms(mem_ref[...], g_ref[...]).astype(BF), w_ref[...])
    k_ref[...] = kv[:, :MEM_W]
    v_ref[...] = kv[:, MEM_W:]


def _memory_kv(mem, g, w):
    n = mem.shape[0]
    return pl.pallas_call(
        _memkv_kernel,
        grid=(1,),
        in_specs=[_const_spec((n, D_MODEL)), _const_spec((1, D_MODEL)),
                  _const_spec((D_MODEL, 2 * MEM_W))],
        out_specs=(_const_spec((n, MEM_W)), _const_spec((n, MEM_W))),
        out_shape=(jax.ShapeDtypeStruct((n, MEM_W), F32),) * 2,
        compiler_params=_params("arbitrary"),
        name="memory_kv",
    )(mem, g, w)


def _softmax_rows(s):
    p = jnp.exp(s - jnp.max(s, axis=1, keepdims=True))
    return p, jnp.sum(p, axis=1, keepdims=True)


def _mem_prompt_kernel(q_ref, k_ref, v_ref, o_ref):
    for hd in range(MEM_HEADS):
        sl = slice(hd * MEM_HEAD_DIM, (hd + 1) * MEM_HEAD_DIM)
        p, l = _softmax_rows(_dot_nt(q_ref[:, sl], k_ref[0, :, sl].astype(BF)))
        o = _dot(p.astype(BF), v_ref[0, :, sl].astype(BF))
        o_ref[:, sl] = (o / l).astype(BF)


def _mem_prompt(qm, mk, mv, seq_len):
    n = qm.shape[0]
    tm = _row_tile(seq_len)
    tps = seq_len // tm
    n_mem = mk.shape[1]
    kv = pl.BlockSpec((1, n_mem, MEM_W), lambda i: (i // tps, 0, 0))
    return pl.pallas_call(
        _mem_prompt_kernel,
        grid=(n // tm,),
        in_specs=[pl.BlockSpec((tm, MEM_W), lambda i: (i, 0)), kv, kv],
        out_specs=pl.BlockSpec((tm, MEM_W), lambda i: (i, 0)),
        out_shape=jax.ShapeDtypeStruct((n, MEM_W), BF),
        compiler_params=_params("parallel"),
        name="mem_prompt",
    )(qm, mk, mv)


def _mem_decode_kernel(q_ref, k_ref, v_ref, o_ref):
    rows, n_kv = q_ref.shape[1], k_ref.shape[1]
    r_head = lax.broadcasted_iota(jnp.int32, (rows, n_kv), 0) & (MEM_HEADS - 1)
    c_head = lax.broadcasted_iota(jnp.int32, (rows, n_kv), 1) & (MEM_HEADS - 1)
    own = r_head == c_head
    n_b = q_ref.shape[0]
    scores = [_dot_nt(q_ref[b], k_ref[b].astype(BF)) for b in range(n_b)]
    for b in range(n_b):
        p, l = _softmax_rows(jnp.where(own, scores[b], NEG_INF))
        o_ref[b] = (_dot(p.astype(BF), v_ref[b].astype(BF)) / l).astype(BF)


def _mem_decode(qm, cache_k, cache_v):
    n_b, n_mem = cache_k.shape[:2]
    assert MEM_HEADS & (MEM_HEADS - 1) == 0
    rows = 4 * MEM_HEADS
    n_kv = n_mem * MEM_HEADS
    g = MEM_DEC_BATCH if n_b % MEM_DEC_BATCH == 0 else n_b
    kv = pl.BlockSpec((g, n_kv, MEM_HEAD_DIM), lambda i: (i, 0, 0))
    qo = pl.BlockSpec((g, rows, MEM_HEAD_DIM), lambda i: (i, 0, 0))
    o = pl.pallas_call(
        _mem_decode_kernel,
        grid=(n_b // g,),
        in_specs=[qo, kv, kv],
        out_specs=qo,
        out_shape=jax.ShapeDtypeStruct((n_b, rows, MEM_HEAD_DIM), BF),
        compiler_params=_params("parallel"),
        name="mem_decode",
    )(qm.reshape(n_b, rows, MEM_HEAD_DIM), cache_k.reshape(n_b, n_kv, MEM_HEAD_DIM),
      cache_v.reshape(n_b, n_kv, MEM_HEAD_DIM))
    return o.reshape(n_b * 4, MEM_W)


def _prep_weights(l, g_ffn1_pre, w_ffn1_gu, w_ffn1_down, g_ffn1_post, g_mix_pre, w_in,
                  b_fgate, g_q_norm, w_q_up, g_kv_norm, w_kv_up, w_out, g_mix_post,
                  g_mem_tok, w_mem_kv, g_mem_pre, w_mem_q, w_mem_o, g_mem_post,
                  g_ffn2_pre, w_ffn2_gu, w_ffn2_down, g_ffn2_post):
    row = lambda g: g[l].reshape(1, -1)
    w = {}
    for name, (gpre, wgu, wd, gpost) in {
            "ffn1": (g_ffn1_pre, w_ffn1_gu, w_ffn1_down, g_ffn1_post),
            "ffn2": (g_ffn2_pre, w_ffn2_gu, w_ffn2_down, g_ffn2_post)}.items():
        w[name] = (row(gpre), wgu[l].astype(BF), wd[l].astype(BF), row(gpost))
    cuts = np.cumsum([FOX_Q_W, FOX_KV_W, FOX_KV_W, FOX_HEADS, MLA_Q_LORA, MLA_KV_LORA]).tolist()
    wq, wk, wv, wf, wcq, wckv, wkr = jnp.split(w_in[l], cuts, axis=1)
    half = MLA_ROPE_DIM // 2
    pad = lambda a: jnp.pad(a, ((0, 0), (0, LANES - a.shape[1])))
    wkr_rot = jnp.concatenate([wkr[:, half:], wkr[:, :half]], axis=1)
    w["g_mix_pre"] = row(g_mix_pre)
    w["w_big"] = jnp.concatenate([wq, wk, wv, wcq, wckv], axis=1).astype(BF)
    w["w_small"] = jnp.concatenate([pad(wkr), pad(wkr_rot), pad(wf)], axis=1).astype(BF)
    w["b_f"] = pad(b_fgate[l].reshape(1, -1))
    w["g_q_norm"] = row(g_q_norm)
    w["g_kv_norm"] = row(g_kv_norm)
    wqu = w_q_up[l].reshape(MLA_Q_LORA, MLA_HEADS, MLA_NOPE_DIM + MLA_ROPE_DIM)
    w["w_q_nope"] = wqu[..., :MLA_NOPE_DIM].reshape(MLA_Q_LORA, -1).astype(BF)
    wqr = wqu[..., MLA_NOPE_DIM:]
    wqr_rot = jnp.concatenate([wqr[..., half:], wqr[..., :half]], axis=-1)
    w["w_q_rope_a"] = wqr.reshape(MLA_Q_LORA, -1).astype(BF)
    w["w_q_rope_b"] = wqr_rot.reshape(MLA_Q_LORA, -1).astype(BF)
    wkv = w_kv_up[l].reshape(MLA_KV_LORA, MLA_HEADS, MLA_NOPE_DIM + MLA_V_DIM)
    w["w_uk_t"] = jnp.transpose(wkv[..., :MLA_NOPE_DIM], (1, 2, 0)).astype(BF)
    w["w_uv"] = jnp.transpose(wkv[..., MLA_NOPE_DIM:], (1, 0, 2)).astype(BF)
    w["w_out"] = w_out[l].astype(BF)
    w["g_mix_post"] = row(g_mix_post)
    w["g_mem_tok"] = row(g_mem_tok)
    w["w_mem_kv"] = w_mem_kv[l].astype(BF)
    w["g_mem_pre"] = row(g_mem_pre)
    w["w_mem_q"] = w_mem_q[l].astype(BF)
    w["w_mem_o"] = w_mem_o[l].astype(BF)
    w["g_mem_post"] = row(g_mem_post)
    return w


def _rope_tables(pos):
    half = MLA_ROPE_DIM // 2
    inv_freq = ROPE_THETA ** (-jnp.arange(half, dtype=F32) / half)
    ang = pos.astype(F32)[:, None] * inv_freq[None, :]
    cos, sin = jnp.cos(ang), jnp.sin(ang)
    cos_t = jnp.tile(jnp.concatenate([cos, cos], axis=1), (1, MLA_HEADS))
    sin_t = jnp.tile(jnp.concatenate([-sin, sin], axis=1), (1, MLA_HEADS))
    return cos_t, sin_t


def kernel(x_prompt, x_sample, mem_prompt, cache_fox_k, cache_fox_v, cache_fox_logf, cache_mla_ckv, cache_mla_krope, cache_mem_k, cache_mem_v, page_table, g_ffn1_pre, w_ffn1_gu, w_ffn1_down, g_ffn1_post, g_mix_pre, w_in, b_fgate, g_q_norm, w_q_up, g_kv_norm, w_kv_up, w_out, g_mix_post, g_mem_tok, w_mem_kv, g_mem_pre, w_mem_q, w_mem_o, g_mem_post, g_ffn2_pre, w_ffn2_gu, w_ffn2_down, g_ffn2_post):
    n_seq, seq_len, _ = x_prompt.shape
    n_dec, dec_seq, _ = x_sample.shape
    depth = w_in.shape[0]
    n_pages = page_table.shape[1]
    past_len = n_pages * PAGE_SIZE
    n_mem = mem_prompt.shape[1]
    assert dec_seq == 4

    xp = x_prompt.reshape(n_seq * seq_len, D_MODEL)
    xs = x_sample.reshape(n_dec * dec_seq, D_MODEL)
    mem = mem_prompt.reshape(n_seq * n_mem, D_MODEL)
    cos_p, sin_p = _rope_tables(jnp.arange(seq_len, dtype=jnp.int32))
    pos_s = past_len + jnp.arange(dec_seq, dtype=jnp.int32)
    cos_s, sin_s = _rope_tables(jnp.tile(pos_s, n_dec))

    outs = {k: [] for k in ("pk", "pv", "plf", "pckv", "pkr", "pmk", "pmv",
                            "sk", "sv", "slf", "sckv", "skr")}
    for l in range(depth):
        w = _prep_weights(l, g_ffn1_pre, w_ffn1_gu, w_ffn1_down, g_ffn1_post, g_mix_pre,
                          w_in, b_fgate, g_q_norm, w_q_up, g_kv_norm, w_kv_up, w_out,
                          g_mix_post, g_mem_tok, w_mem_kv, g_mem_pre, w_mem_q, w_mem_o,
                          g_mem_post, g_ffn2_pre, w_ffn2_gu, w_ffn2_down, g_ffn2_post)

        xp = _ffn_half(xp, *w["ffn1"])
        pp = _mix_proj(xp, seq_len, w, cos_p, sin_p, prompt=True)
        o_fox = _fox_prompt(pp, n_seq, seq_len)
        o_lat = _mla_prompt(pp, n_seq, seq_len)
        xp, qm = _mix_out(xp, o_fox, o_lat, w)
        mk, mv = _memory_kv(mem, w["g_mem_tok"], w["w_mem_kv"])
        om = _mem_prompt(qm, mk.reshape(n_seq, n_mem, MEM_W), mv.reshape(n_seq, n_mem, MEM_W),
                         seq_len)
        xp = _memout_ffn(xp, om, w["w_mem_o"], w["g_mem_post"], *w["ffn2"])
        heads_last = lambda a: jnp.transpose(
            a.reshape(n_seq, FOX_KV_HEADS, FOX_HEAD_DIM, seq_len), (0, 3, 1, 2))
        outs["pk"].append(heads_last(pp["kT32"]))
        outs["pv"].append(heads_last(pp["vT32"]))
        outs["plf"].append(jnp.transpose(pp["lfT"], (0, 2, 1)))
        outs["pckv"].append(pp["ckv32"].reshape(n_seq, seq_len, MLA_KV_LORA))
        outs["pkr"].append(jnp.transpose(pp["krT32"], (0, 2, 1)))
        outs["pmk"].append(mk.reshape(n_seq, n_mem, MEM_HEADS, MEM_HEAD_DIM))
        outs["pmv"].append(mv.reshape(n_seq, n_mem, MEM_HEADS, MEM_HEAD_DIM))

        xs = _ffn_half(xs, *w["ffn1"])
        ps = _mix_proj(xs, n_dec * dec_seq, w, cos_s, sin_s, prompt=False)
        n_pool = cache_fox_k.shape[1]
        kc = jnp.transpose(cache_fox_k[l], (0, 2, 3, 1)).reshape(n_pool, FOX_KV_W, PAGE_SIZE)
        vc = jnp.transpose(cache_fox_v[l], (0, 2, 3, 1)).reshape(n_pool, FOX_KV_W, PAGE_SIZE)
        krc = jnp.transpose(cache_mla_krope[l], (0, 2, 1))
        lfc = jnp.transpose(cache_fox_logf[l], (0, 2, 1))
        o_fox, o_lat = _decode_attention(ps, page_table, kc, vc, cache_mla_ckv[l], krc, lfc)
        xs, qm = _mix_out(xs, o_fox, o_lat, w)
        om = _mem_decode(qm, cache_mem_k[l], cache_mem_v[l])
        xs = _memout_ffn(xs, om, w["w_mem_o"], w["g_mem_post"], *w["ffn2"])
        outs["sk"].append(ps["k32"].reshape(n_dec, dec_seq, FOX_KV_HEADS, FOX_HEAD_DIM))
        outs["sv"].append(ps["v32"].reshape(n_dec, dec_seq, FOX_KV_HEADS, FOX_HEAD_DIM))
        outs["slf"].append(ps["lf"].reshape(n_dec, dec_seq, FOX_HEADS))
        outs["sckv"].append(ps["ckv32"].reshape(n_dec, dec_seq, MLA_KV_LORA))
        outs["skr"].append(ps["kr32"].reshape(n_dec, dec_seq, MLA_ROPE_DIM))

    st = {k: jnp.stack(v) for k, v in outs.items()}
    return (xp.reshape(n_seq, seq_len, D_MODEL), xs.reshape(n_dec, dec_seq, D_MODEL),
            st["pk"], st["pv"], st["plf"], st["pckv"], st["pkr"], st["pmk"], st["pmv"],
            st["sk"], st["sv"], st["slf"], st["sckv"], st["skr"])
```

```python
import functools

import numpy as np
import jax
import jax.numpy as jnp
from jax import lax
from jax.experimental import pallas as pl
from jax.experimental.pallas import tpu as pltpu

BF = jnp.bfloat16
F32 = jnp.float32

D_MODEL = 1024
PAGE_SIZE = 128
FOX_HEADS = 8
FOX_KV_HEADS = 4
FOX_GROUP = FOX_HEADS // FOX_KV_HEADS
FOX_HEAD_DIM = 64
FOX_Q_W = FOX_HEADS * FOX_HEAD_DIM
FOX_KV_W = FOX_KV_HEADS * FOX_HEAD_DIM
MLA_HEADS = 4
MLA_Q_LORA = 256
MLA_KV_LORA = 256
MLA_NOPE_DIM = 128
MLA_ROPE_DIM = 64
MLA_V_DIM = 128
ROPE_THETA = 10000.0
MEM_HEADS = 4
MEM_HEAD_DIM = 128
MEM_W = MEM_HEADS * MEM_HEAD_DIM
D_FF = 2816
RMS_EPS = 1e-6
NEG_INF = -1e30

LANES = 128
VMEM_LIMIT_BYTES = 56 * 1024 * 1024

LOG2E = 1.4426950408889634
FOX_SCALE = FOX_HEAD_DIM ** -0.5 * LOG2E
MLA_SCALE = (MLA_NOPE_DIM + MLA_ROPE_DIM) ** -0.5 * LOG2E
MEM_SCALE = MEM_HEAD_DIM ** -0.5

ROW_TILE = 512
FF_CHUNK = 256
ATT_TQ = 512
ATT_TK = 1024
DEC_PAGES = 32
MEM_DEC_BATCH = 8


def _params(*sem):
    return pltpu.CompilerParams(dimension_semantics=sem,
                                vmem_limit_bytes=VMEM_LIMIT_BYTES)


def _rms(x, g):
    return x * lax.rsqrt(jnp.mean(x * x, axis=-1, keepdims=True) + RMS_EPS) * g


def _dot(a, b):
    return jnp.dot(a, b, preferred_element_type=F32)


def _dot_nt(a, b):
    return lax.dot_general(a, b, (((1,), (1,)), ((), ())), preferred_element_type=F32)


def _const_spec(shape):
    zeros = (0,) * len(shape)
    return pl.BlockSpec(shape, lambda *_: zeros)


def _row_tile(n):
    return ROW_TILE if n % ROW_TILE == 0 else n


def _swiglu_half(x, gpre, wgu_ref, wd_ref, gpost):
    h = _rms(x, gpre).astype(BF)
    acc = jnp.zeros(x.shape, F32)
    for c in range(D_FF // FF_CHUNK):
        g = _dot(h, wgu_ref[:, c * FF_CHUNK:(c + 1) * FF_CHUNK])
        u = _dot(h, wgu_ref[:, D_FF + c * FF_CHUNK:D_FF + (c + 1) * FF_CHUNK])
        a = (g * jax.nn.sigmoid(g) * u).astype(BF)
        acc = acc + _dot(a, wd_ref[c * FF_CHUNK:(c + 1) * FF_CHUNK, :])
    return x + 0.5 * _rms(acc, gpost)


def _ffn_kernel(x_ref, gpre_ref, wgu_ref, wd_ref, gpost_ref, o_ref):
    o_ref[...] = _swiglu_half(x_ref[...], gpre_ref[...], wgu_ref, wd_ref, gpost_ref[...])


def _ffn_half(x, gpre, wgu, wd, gpost):
    n = x.shape[0]
    tm = _row_tile(n)
    row = pl.BlockSpec((tm, D_MODEL), lambda i: (i, 0))
    return pl.pallas_call(
        _ffn_kernel,
        grid=(n // tm,),
        in_specs=[row, _const_spec((1, D_MODEL)), _const_spec((D_MODEL, 2 * D_FF)),
                  _const_spec((D_FF, D_MODEL)), _const_spec((1, D_MODEL))],
        out_specs=row,
        out_shape=jax.ShapeDtypeStruct((n, D_MODEL), F32),
        compiler_params=_params("parallel"),
        name="ffn_half",
    )(x, gpre, wgu, wd, gpost)


def _memout_ffn_kernel(x_ref, om_ref, wmo_ref, gmpost_ref, gpre_ref, wgu_ref, wd_ref, gpost_ref,
                       o_ref):
    x = x_ref[...] + _rms(_dot(om_ref[...], wmo_ref[...]), gmpost_ref[...])
    o_ref[...] = _swiglu_half(x, gpre_ref[...], wgu_ref, wd_ref, gpost_ref[...])


def _memout_ffn(x, om, wmo, gmpost, gpre, wgu, wd, gpost):
    n = x.shape[0]
    tm = _row_tile(n)
    row = pl.BlockSpec((tm, D_MODEL), lambda i: (i, 0))
    return pl.pallas_call(
        _memout_ffn_kernel,
        grid=(n // tm,),
        in_specs=[row, pl.BlockSpec((tm, MEM_W), lambda i: (i, 0)),
                  _const_spec((MEM_W, D_MODEL)), _const_spec((1, D_MODEL)),
                  _const_spec((1, D_MODEL)), _const_spec((D_MODEL, 2 * D_FF)),
                  _const_spec((D_FF, D_MODEL)), _const_spec((1, D_MODEL))],
        out_specs=row,
        out_shape=jax.ShapeDtypeStruct((n, D_MODEL), F32),
        compiler_params=_params("parallel"),
        name="memout_ffn",
    )(x, om, wmo, gmpost, gpre, wgu, wd, gpost)


BIG_W = FOX_Q_W + 2 * FOX_KV_W + MLA_Q_LORA + MLA_KV_LORA
SMALL_W = 3 * LANES


def _log_sigmoid(x):
    return jnp.minimum(x, 0.0) - jnp.log1p(jnp.exp(-jnp.abs(x)))


MIX_COMMON = ("ckv32",)
MIX_PROMPT = ("kT32", "vT32", "krT32", "lfT", "qT", "k16", "vT1", "qlatT", "qropeT", "ckv16",
              "ckvT1", "kr16", "c", "cT")
MIX_DECODE = ("q", "qlat", "qrope", "k32", "v32", "kr32", "lf")
MLA_VT_ROWS = MLA_KV_LORA + 16
FOX_VT_ROWS = FOX_HEAD_DIM + 16


def _mixproj_kernel(x_ref, g_ref, wbig_ref, wsmall_ref, bf_ref, gq_ref, gkv_ref,
                    wqn_ref, wqra_ref, wqrb_ref, wuk_ref, cos_ref, sin_ref, *rest,
                    names, tiles_per_seq):
    o = dict(zip(names, rest))
    carry_ref = rest[-1]
    prompt = "qT" in o
    i = pl.program_id(0)
    tm = x_ref.shape[0]
    h = _rms(x_ref[...], g_ref[...]).astype(BF)
    big = _dot(h, wbig_ref[...])
    small = _dot(h, wsmall_ref[...])
    cos = cos_ref[...]
    sin = sin_ref[...]

    q = big[:, :FOX_Q_W] * FOX_SCALE
    k = big[:, FOX_Q_W:FOX_Q_W + FOX_KV_W]
    v = big[:, FOX_Q_W + FOX_KV_W:FOX_Q_W + 2 * FOX_KV_W]
    lf = _log_sigmoid(small[:, 2 * LANES:] + bf_ref[...])
    if prompt:
        o["kT32"][0] = k.T
        o["k16"][...] = k.astype(BF)
        vT = v.T
        o["vT32"][0] = vT
        ones_row = (lax.broadcasted_iota(jnp.int32, (FOX_VT_ROWS - FOX_HEAD_DIM, tm), 0) == 0).astype(BF)
        for kvh in range(FOX_KV_HEADS):
            o["vT1"][0, kvh * FOX_VT_ROWS:kvh * FOX_VT_ROWS + FOX_HEAD_DIM] = (
                vT[kvh * FOX_HEAD_DIM:(kvh + 1) * FOX_HEAD_DIM].astype(BF))
            o["vT1"][0, kvh * FOX_VT_ROWS + FOX_HEAD_DIM:(kvh + 1) * FOX_VT_ROWS] = ones_row
        qT = q.T.astype(BF)
        zero = jnp.zeros((FOX_HEAD_DIM, tm), BF)
        for hh in range(FOX_HEADS):
            piece = qT[hh * FOX_HEAD_DIM:(hh + 1) * FOX_HEAD_DIM]
            odd = (hh // FOX_GROUP) % 2 == 1
            o["qT"][0, hh * LANES:(hh + 1) * LANES] = jnp.concatenate(
                [zero, piece] if odd else [piece, zero], axis=0)

        lfT = lf.T
        o["lfT"][0] = lfT[:FOX_HEADS]

        @pl.when(i % tiles_per_seq == 0)
        def _():
            carry_ref[...] = jnp.zeros_like(carry_ref)

        lane = lax.broadcasted_iota(jnp.int32, lfT.shape, 1)
        run = lfT
        sh = 1
        while sh < tm:
            run = run + jnp.where(lane >= sh, pltpu.roll(run, sh, axis=1), 0.0)
            sh *= 2
        run = run + carry_ref[...]
        carry_ref[...] = run[:, tm - 1:tm]
        run2 = run * LOG2E
        o["cT"][0] = run2[:FOX_HEADS]
        o["c"][...] = run2.T[:, :FOX_HEADS]
    else:
        o["q"][...] = q.astype(BF)
        o["k32"][...] = k
        o["v32"][...] = v
        o["lf"][...] = lf[:, :FOX_HEADS]

    cq = big[:, FOX_Q_W + 2 * FOX_KV_W:FOX_Q_W + 2 * FOX_KV_W + MLA_Q_LORA]
    cqn = _rms(cq, gq_ref[...]).astype(BF)
    qn = _dot(cqn, wqn_ref[...]).astype(BF)
    qr = (_dot(cqn, wqra_ref[...]) * cos + _dot(cqn, wqrb_ref[...]) * sin) * MLA_SCALE
    if prompt:
        qrT = qr.T.astype(BF)
        zero = jnp.zeros((LANES - MLA_ROPE_DIM, tm), BF)
    else:
        o["qrope"][...] = qr.astype(BF)
    for hd in range(MLA_HEADS):
        ql = _dot(qn[:, hd * MLA_NOPE_DIM:(hd + 1) * MLA_NOPE_DIM], wuk_ref[hd]) * MLA_SCALE
        if prompt:
            o["qlatT"][0, hd * MLA_KV_LORA:(hd + 1) * MLA_KV_LORA] = ql.T.astype(BF)
            o["qropeT"][0, hd * LANES:(hd + 1) * LANES] = jnp.concatenate(
                [qrT[hd * MLA_ROPE_DIM:(hd + 1) * MLA_ROPE_DIM], zero], axis=0)
        else:
            o["qlat"][:, hd * MLA_KV_LORA:(hd + 1) * MLA_KV_LORA] = ql.astype(BF)

    ckv = _rms(big[:, BIG_W - MLA_KV_LORA:], gkv_ref[...])
    o["ckv32"][...] = ckv
    kr = small[:, :LANES] * cos[:, :LANES] + small[:, LANES:2 * LANES] * sin[:, :LANES]
    if prompt:
        o["ckv16"][...] = ckv.astype(BF)
        o["ckvT1"][0, :MLA_KV_LORA] = ckv.T.astype(BF)
        o["ckvT1"][0, MLA_KV_LORA:] = (
            lax.broadcasted_iota(jnp.int32, (MLA_VT_ROWS - MLA_KV_LORA, tm), 0) == 0).astype(BF)
        o["krT32"][0] = kr.T[:MLA_ROPE_DIM]
        o["kr16"][...] = kr.astype(BF)
    else:
        o["kr32"][...] = kr[:, :MLA_ROPE_DIM]


def _mix_proj(x, seq_len, w, cos_tab, sin_tab, prompt):
    n = x.shape[0]
    tm = _row_tile(min(n, seq_len))
    n_seq = n // seq_len
    tps = seq_len // tm
    tab_tiles = cos_tab.shape[0] // tm
    row = lambda width: pl.BlockSpec((tm, width), lambda i: (i, 0))
    colT = lambda height: pl.BlockSpec((1, height, tm), lambda i: (i // tps, 0, i % tps))
    tab = pl.BlockSpec((tm, 2 * LANES), lambda i: (i % tab_tiles, 0))
    rows = lambda width, dt: (jax.ShapeDtypeStruct((n, width), dt), row(width))
    cols = lambda height, dt: (jax.ShapeDtypeStruct((n_seq, height, seq_len), dt), colT(height))
    outputs = {
        "q": rows(FOX_Q_W, BF), "qlat": rows(MLA_HEADS * MLA_KV_LORA, BF),
        "qrope": rows(MLA_HEADS * MLA_ROPE_DIM, BF), "ckv32": rows(MLA_KV_LORA, F32),
        "kT32": cols(FOX_KV_W, F32), "vT32": cols(FOX_KV_W, F32),
        "krT32": cols(MLA_ROPE_DIM, F32), "lfT": cols(FOX_HEADS, F32),
        "qT": cols(FOX_HEADS * LANES, BF), "k16": rows(FOX_KV_W, BF),
        "vT1": cols(FOX_KV_HEADS * FOX_VT_ROWS, BF),
        "ckv16": rows(MLA_KV_LORA, BF), "ckvT1": cols(MLA_VT_ROWS, BF),
        "kr16": rows(LANES, BF), "qlatT": cols(MLA_HEADS * MLA_KV_LORA, BF),
        "qropeT": cols(MLA_HEADS * LANES, BF), "c": rows(FOX_HEADS, F32), "cT": cols(FOX_HEADS, F32),
        "k32": rows(FOX_KV_W, F32), "v32": rows(FOX_KV_W, F32),
        "kr32": rows(MLA_ROPE_DIM, F32), "lf": rows(FOX_HEADS, F32),
    }
    names = MIX_COMMON + (MIX_PROMPT if prompt else MIX_DECODE)
    out_shape = tuple(outputs[k][0] for k in names)
    out_specs = tuple(outputs[k][1] for k in names)
    in_specs = [
        row(D_MODEL), _const_spec((1, D_MODEL)), _const_spec((D_MODEL, BIG_W)),
        _const_spec((D_MODEL, SMALL_W)), _const_spec((1, LANES)),
        _const_spec((1, MLA_Q_LORA)), _const_spec((1, MLA_KV_LORA)),
        _const_spec((MLA_Q_LORA, MLA_HEADS * MLA_NOPE_DIM)),
        _const_spec((MLA_Q_LORA, MLA_HEADS * MLA_ROPE_DIM)),
        _const_spec((MLA_Q_LORA, MLA_HEADS * MLA_ROPE_DIM)),
        _const_spec((MLA_HEADS, MLA_NOPE_DIM, MLA_KV_LORA)), tab, tab,
    ]
    outs = pl.pallas_call(
        functools.partial(_mixproj_kernel, names=names, tiles_per_seq=tps),
        grid=(n // tm,),
        in_specs=in_specs,
        out_specs=out_specs,
        out_shape=out_shape,
        scratch_shapes=[pltpu.VMEM((LANES, 1), F32)],
        compiler_params=_params("arbitrary"),
        name="mix_proj",
    )(x, w["g_mix_pre"], w["w_big"], w["w_small"], w["b_f"], w["g_q_norm"],
      w["g_kv_norm"], w["w_q_nope"], w["w_q_rope_a"], w["w_q_rope_b"], w["w_uk_t"],
      cos_tab, sin_tab)
    return dict(zip(names, outs))


def _online_update(m_ref, l_ref, acc_ref, h, z, v, row_bias=None):
    m_prev = m_ref[h]
    m_z = jnp.max(z, axis=1, keepdims=True)
    if row_bias is None:
        m_new = jnp.maximum(m_prev, m_z)
        shift = m_new
    else:
        m_new = jnp.maximum(m_prev, m_z + row_bias)
        shift = m_new - row_bias
    alpha = jnp.exp2(m_prev - m_new)
    p = jnp.exp2(z - shift)
    if l_ref is not None:
        l_ref[h] = alpha * l_ref[h] + jnp.sum(p, axis=1, keepdims=True)
    acc_ref[h] = alpha * acc_ref[h] + _dot(p.astype(BF), v)
    m_ref[h] = m_new


def _init_state(m_ref, l_ref, acc_ref):
    m_ref[...] = jnp.full_like(m_ref, NEG_INF)
    if l_ref is not None:
        l_ref[...] = jnp.zeros_like(l_ref)
    acc_ref[...] = jnp.zeros_like(acc_ref)


def _causal_keep(shape, q0, k0, q_axis):
    qpos = q0 + lax.broadcasted_iota(jnp.int32, shape, q_axis)
    kpos = k0 + lax.broadcasted_iota(jnp.int32, shape, 1 - q_axis)
    return qpos >= kpos


def _sweep_key_blocks(block, i, tq, q_axis=0):
    assert ATT_TK == 2 * tq
    q0 = i * tq
    n_full = q0 // ATT_TK

    def body(kj, carry):
        block(pl.multiple_of(kj * ATT_TK, ATT_TK), ATT_TK, None)
        return carry

    lax.fori_loop(0, n_full, body, 0)
    k0 = pl.multiple_of(n_full * ATT_TK, ATT_TK)

    def last(width):
        shape = (tq, width) if q_axis == 0 else (width, tq)
        block(k0, width, _causal_keep(shape, q0, k0, q_axis))

    pl.when(q0 == k0)(lambda: last(tq))
    pl.when(q0 != k0)(lambda: last(ATT_TK))


def _fox_prompt_kernel(qT_ref, cq_ref, k_ref, vT1_ref, ck_ref, o_ref, m_ref, acc_ref):
    i = pl.program_id(1)
    tq = qT_ref.shape[2]
    _init_state(m_ref, None, acc_ref)

    def block(k0, width, keep):
        keys = pl.ds(k0, width)

        def scores(hh):
            pair = hh // (2 * FOX_GROUP)
            kq = _dot(k_ref[keys, pair * LANES:(pair + 1) * LANES], qT_ref[0, hh * LANES:(hh + 1) * LANES, :])
            return kq - ck_ref[keys, hh:hh + 1]

        z = scores(0)
        for hh in range(FOX_HEADS):
            z_next = scores(hh + 1) if hh + 1 < FOX_HEADS else None
            if keep is not None:
                z = jnp.where(keep, z, NEG_INF)
            kvh = hh // FOX_GROUP
            cq = cq_ref[0, hh:hh + 1, :]
            m_prev = m_ref[hh]
            m_new = jnp.maximum(m_prev, jnp.max(z, axis=0, keepdims=True) + cq)
            alpha = jnp.exp2(m_prev - m_new)
            p = jnp.exp2(z - (m_new - cq))
            vT1 = vT1_ref[0, kvh * FOX_VT_ROWS:(kvh + 1) * FOX_VT_ROWS, keys]
            acc_ref[hh] = alpha * acc_ref[hh] + _dot(vT1, p.astype(BF))
            m_ref[hh] = m_new
            z = z_next

    _sweep_key_blocks(block, i, tq, q_axis=1)
    for hh in range(FOX_HEADS):
        acc = acc_ref[hh]
        oT = acc[:FOX_HEAD_DIM] / acc[FOX_HEAD_DIM:FOX_HEAD_DIM + 1]
        o_ref[:, hh * FOX_HEAD_DIM:(hh + 1) * FOX_HEAD_DIM] = oT.T.astype(BF)


def _fox_prompt(p, n_seq, seq_len):
    tq = ATT_TQ
    nq = seq_len // tq
    return pl.pallas_call(
        _fox_prompt_kernel,
        grid=(n_seq, nq),
        in_specs=[
            pl.BlockSpec((1, FOX_HEADS * LANES, tq), lambda b, i: (b, 0, i)),
            pl.BlockSpec((1, FOX_HEADS, tq), lambda b, i: (b, 0, i)),
            pl.BlockSpec((seq_len, FOX_KV_W), lambda b, i: (b, 0)),
            pl.BlockSpec((1, FOX_KV_HEADS * FOX_VT_ROWS, seq_len), lambda b, i: (b, 0, 0)),
            pl.BlockSpec((seq_len, FOX_HEADS), lambda b, i: (b, 0)),
        ],
        out_specs=pl.BlockSpec((tq, FOX_Q_W), lambda b, i: (b * nq + i, 0)),
        out_shape=jax.ShapeDtypeStruct((n_seq * seq_len, FOX_Q_W), BF),
        scratch_shapes=[pltpu.VMEM((FOX_HEADS, 1, tq), F32),
                        pltpu.VMEM((FOX_HEADS, FOX_VT_ROWS, tq), F32)],
        compiler_params=_params("parallel", "arbitrary"),
        name="fox_prompt",
    )(p["qT"], p["cT"], p["k16"], p["vT1"], p["c"])


def _mla_prompt_kernel(qlT_ref, qrT_ref, ckv_ref, kr_ref, ckvT1_ref, o_ref, m_ref, acc_ref):
    i = pl.program_id(1)
    tq = qlT_ref.shape[2]
    _init_state(m_ref, None, acc_ref)

    def block(k0, width, keep):
        keys = pl.ds(k0, width)
        ckv = ckv_ref[keys, :]
        kr = kr_ref[keys, :]
        ckvT1 = ckvT1_ref[0, :, keys]

        def scores(hd):
            return (_dot(ckv, qlT_ref[0, hd * MLA_KV_LORA:(hd + 1) * MLA_KV_LORA, :])
                    + _dot(kr, qrT_ref[0, hd * LANES:(hd + 1) * LANES, :]))

        z = scores(0)
        for hd in range(MLA_HEADS):
            z_next = scores(hd + 1) if hd + 1 < MLA_HEADS else None
            if keep is not None:
                z = jnp.where(keep, z, NEG_INF)
            m_prev = m_ref[hd]
            m_new = jnp.maximum(m_prev, jnp.max(z, axis=0, keepdims=True))
            alpha = jnp.exp2(m_prev - m_new)
            p = jnp.exp2(z - m_new)
            acc_ref[hd] = alpha * acc_ref[hd] + _dot(ckvT1, p.astype(BF))
            m_ref[hd] = m_new
            z = z_next

    _sweep_key_blocks(block, i, tq, q_axis=1)
    for hd in range(MLA_HEADS):
        acc = acc_ref[hd]
        oT = acc[:MLA_KV_LORA] / acc[MLA_KV_LORA:MLA_KV_LORA + 1]
        o_ref[:, hd * MLA_KV_LORA:(hd + 1) * MLA_KV_LORA] = oT.T.astype(BF)


def _mla_prompt(p, n_seq, seq_len):
    tq = ATT_TQ
    nq = seq_len // tq
    lat_w = MLA_HEADS * MLA_KV_LORA
    return pl.pallas_call(
        _mla_prompt_kernel,
        grid=(n_seq, nq),
        in_specs=[
            pl.BlockSpec((1, lat_w, tq), lambda b, i: (b, 0, i)),
            pl.BlockSpec((1, MLA_HEADS * LANES, tq), lambda b, i: (b, 0, i)),
            pl.BlockSpec((seq_len, MLA_KV_LORA), lambda b, i: (b, 0)),
            pl.BlockSpec((seq_len, LANES), lambda b, i: (b, 0)),
            pl.BlockSpec((1, MLA_VT_ROWS, seq_len), lambda b, i: (b, 0, 0)),
        ],
        out_specs=pl.BlockSpec((tq, lat_w), lambda b, i: (b * nq + i, 0)),
        out_shape=jax.ShapeDtypeStruct((n_seq * seq_len, lat_w), BF),
        scratch_shapes=[pltpu.VMEM((MLA_HEADS, 1, tq), F32),
                        pltpu.VMEM((MLA_HEADS, MLA_VT_ROWS, tq), F32)],
        compiler_params=_params("parallel", "arbitrary"),
        name="mla_prompt",
    )(p["qlatT"], p["qropeT"], p["ckv16"], p["kr16"], p["ckvT1"])


DEC_FOX_ROWS = 4 * FOX_HEADS
DEC_MLA_ROWS = 4 * MLA_HEADS


def _decode_kernel(pt_ref, qbd_ref, lfnew_ref, knew_ref, vnew_ref, qlat_ref, qrope_ref,
                   ckvnew_ref, krnew_ref, kc_ref, vc_ref, ckvc_ref, krc_ref, lfc_ref,
                   of_ref, om_ref,
                   kbuf, vbuf, ckvbuf, krbuf, lfbuf, sems,
                   mf_ref, lf_ref, accf_ref, mm_ref, lm_ref, accm_ref, rcarry_ref,
                   *, n_chunks, n_steps):
    t = pl.program_id(0)
    n_keys = DEC_PAGES * PAGE_SIZE

    def chunk_copies(step, slot):
        b = step // n_chunks
        first = (n_chunks - 1 - step % n_chunks) * DEC_PAGES
        copies = []
        for j in range(DEC_PAGES):
            pid = pt_ref[b, first + j]
            lanes = pl.ds(j * PAGE_SIZE, PAGE_SIZE)
            copies += [
                pltpu.make_async_copy(kc_ref.at[pid], kbuf.at[slot, :, lanes], sems.at[slot, 0]),
                pltpu.make_async_copy(vc_ref.at[pid], vbuf.at[slot, :, lanes], sems.at[slot, 1]),
                pltpu.make_async_copy(ckvc_ref.at[pid], ckvbuf.at[slot, lanes, :], sems.at[slot, 2]),
                pltpu.make_async_copy(krc_ref.at[pid], krbuf.at[slot, :, lanes], sems.at[slot, 3]),
                pltpu.make_async_copy(lfc_ref.at[pid], lfbuf.at[slot, :, lanes], sems.at[slot, 4]),
            ]
        return copies

    slot = t % 2

    @pl.when(t == 0)
    def _():
        for cp in chunk_copies(t, slot):
            cp.start()

    @pl.when(t + 1 < n_steps)
    def _():
        for cp in chunk_copies(t + 1, 1 - slot):
            cp.start()

    @pl.when(t % n_chunks == 0)
    def _():
        mf_ref[...] = jnp.full_like(mf_ref, NEG_INF)
        lf_ref[...] = jnp.zeros_like(lf_ref)
        accf_ref[...] = jnp.zeros_like(accf_ref)
        mm_ref[...] = jnp.full_like(mm_ref, NEG_INF)
        lm_ref[...] = jnp.zeros_like(lm_ref)
        accm_ref[...] = jnp.zeros_like(accm_ref)
        rcarry_ref[...] = jnp.zeros_like(rcarry_ref)

    for cp in chunk_copies(t, slot):
        cp.wait()

    lfnew = lfnew_ref[0]
    parts = [lfnew[0:FOX_HEADS]]
    for u in range(1, 4):
        parts.append(parts[-1] + lfnew[u * FOX_HEADS:(u + 1) * FOX_HEADS])
    ncol = jnp.concatenate(parts, axis=0) * LOG2E
    qbd = qbd_ref[0]

    lfp = lfbuf[slot]
    lane = lax.broadcasted_iota(jnp.int32, lfp.shape, 1)
    run = lfp
    sh = 1
    while sh < n_keys:
        run = run + jnp.where(lane < n_keys - sh, pltpu.roll(run, n_keys - sh, axis=1), 0.0)
        sh *= 2
    later = run - lfp + rcarry_ref[...]
    rcarry_ref[...] = rcarry_ref[...] + run[:, 0:1]
    bias = jnp.concatenate([later * LOG2E] * 4, axis=0) + ncol

    def update(m_ref, l_ref, acc_ref, s, pv_fn):
        m_prev = m_ref[...]
        m_new = jnp.maximum(m_prev, jnp.max(s, axis=1, keepdims=True))
        alpha = jnp.exp2(m_prev - m_new)
        p = jnp.exp2(s - m_new)
        l_ref[...] = alpha * l_ref[...] + jnp.sum(p, axis=1, keepdims=True)
        acc_ref[...] = alpha * acc_ref[...] + pv_fn(p)
        m_ref[...] = m_new

    s_f = _dot(qbd, kbuf[slot].astype(BF)) + bias
    ckv = ckvbuf[slot].astype(BF)
    qlat = qlat_ref[0]
    qrope = qrope_ref[0]
    s_m = _dot_nt(qlat, ckv) + _dot(qrope, krbuf[slot].astype(BF))
    update(mf_ref, lf_ref, accf_ref, s_f,
           lambda p: _dot_nt(p.astype(BF), vbuf[slot].astype(BF)))
    update(mm_ref, lm_ref, accm_ref, s_m, lambda p: _dot(p.astype(BF), ckv))

    @pl.when(t % n_chunks == n_chunks - 1)
    def _():
        qf = qbd.astype(F32)
        rowf = lax.broadcasted_iota(jnp.int32, (DEC_FOX_ROWS, 1), 0)
        knew = knew_ref[0]
        vnew = vnew_ref[0]
        for u in range(4):
            s = jnp.sum(qf * knew[u:u + 1, :], axis=1, keepdims=True)
            n_u = jnp.concatenate([ncol[u * FOX_HEADS:(u + 1) * FOX_HEADS]] * 4, axis=0)
            s = jnp.where(rowf >= u * FOX_HEADS, s + (ncol - n_u), NEG_INF)
            update(mf_ref, lf_ref, accf_ref, s, lambda p: p * vnew[u:u + 1, :])
        o = accf_ref[...] / lf_ref[...]
        row = lax.broadcasted_iota(jnp.int32, o.shape, 0)
        col = lax.broadcasted_iota(jnp.int32, o.shape, 1)
        o = jnp.where(col // FOX_HEAD_DIM == (row % FOX_HEADS) // FOX_GROUP, o, 0.0)
        of_ref[0] = (o[:, 0:64] + o[:, 64:128]) + (o[:, 128:192] + o[:, 192:256])

        qlf = qlat.astype(F32)
        qrf = qrope.astype(F32)
        rowm = lax.broadcasted_iota(jnp.int32, (DEC_MLA_ROWS, 1), 0)
        ckvnew = ckvnew_ref[0]
        krnew = krnew_ref[0]
        for u in range(4):
            s = (jnp.sum(qlf * ckvnew[u:u + 1, :], axis=1, keepdims=True)
                 + jnp.sum(qrf * krnew[u:u + 1, :], axis=1, keepdims=True))
            s = jnp.where(rowm >= u * MLA_HEADS, s, NEG_INF)
            update(mm_ref, lm_ref, accm_ref, s, lambda p: p * ckvnew[u:u + 1, :])
        om_ref[0] = accm_ref[...] / lm_ref[...]


def _decode_attention(ps, page_table, kc, vc, ckvc, krc, lfc):
    n_b, n_pages = page_table.shape
    n_chunks = n_pages // DEC_PAGES
    n_steps = n_b * n_chunks
    n_keys = DEC_PAGES * PAGE_SIZE

    q = ps["q"].reshape(n_b, 4, FOX_KV_HEADS, FOX_GROUP, FOX_HEAD_DIM)
    eye = jnp.eye(FOX_KV_HEADS, dtype=BF)
    qbd = jnp.einsum("btkgd,kj->btkgjd", q, eye).reshape(n_b, DEC_FOX_ROWS, FOX_KV_W)
    lfnew = ps["lf"].reshape(n_b, DEC_FOX_ROWS, 1)
    knew = ps["k32"].reshape(n_b, 4, FOX_KV_W)
    vnew = ps["v32"].reshape(n_b, 4, FOX_KV_W)
    qlat = ps["qlat"].reshape(n_b, DEC_MLA_ROWS, MLA_KV_LORA)
    qrope = ps["qrope"].reshape(n_b, DEC_MLA_ROWS, MLA_ROPE_DIM)
    ckvnew = ps["ckv32"].reshape(n_b, 4, MLA_KV_LORA)
    krnew = ps["kr32"].reshape(n_b, 4, MLA_ROPE_DIM)

    per_b = lambda r, w: pl.BlockSpec((1, r, w), lambda t, pt: (t // n_chunks, 0, 0))
    hbm = pl.BlockSpec(memory_space=pl.ANY)
    grid_spec = pltpu.PrefetchScalarGridSpec(
        num_scalar_prefetch=1,
        grid=(n_steps,),
        in_specs=[per_b(DEC_FOX_ROWS, FOX_KV_W), per_b(DEC_FOX_ROWS, 1),
                  per_b(4, FOX_KV_W), per_b(4, FOX_KV_W),
                  per_b(DEC_MLA_ROWS, MLA_KV_LORA), per_b(DEC_MLA_ROWS, MLA_ROPE_DIM),
                  per_b(4, MLA_KV_LORA), per_b(4, MLA_ROPE_DIM),
                  hbm, hbm, hbm, hbm, hbm],
        out_specs=[per_b(DEC_FOX_ROWS, FOX_HEAD_DIM), per_b(DEC_MLA_ROWS, MLA_KV_LORA)],
        scratch_shapes=[
            pltpu.VMEM((2, FOX_KV_W, n_keys), F32),
            pltpu.VMEM((2, FOX_KV_W, n_keys), F32),
            pltpu.VMEM((2, n_keys, MLA_KV_LORA), F32),
            pltpu.VMEM((2, MLA_ROPE_DIM, n_keys), F32),
            pltpu.VMEM((2, FOX_HEADS, n_keys), F32),
            pltpu.SemaphoreType.DMA((2, 5)),
            pltpu.VMEM((DEC_FOX_ROWS, 1), F32), pltpu.VMEM((DEC_FOX_ROWS, 1), F32),
            pltpu.VMEM((DEC_FOX_ROWS, FOX_KV_W), F32),
            pltpu.VMEM((DEC_MLA_ROWS, 1), F32), pltpu.VMEM((DEC_MLA_ROWS, 1), F32),
            pltpu.VMEM((DEC_MLA_ROWS, MLA_KV_LORA), F32),
            pltpu.VMEM((FOX_HEADS, 1), F32),
        ],
    )
    o_fox, o_lat = pl.pallas_call(
        functools.partial(_decode_kernel, n_chunks=n_chunks, n_steps=n_steps),
        grid_spec=grid_spec,
        out_shape=(jax.ShapeDtypeStruct((n_b, DEC_FOX_ROWS, FOX_HEAD_DIM), F32),
                   jax.ShapeDtypeStruct((n_b, DEC_MLA_ROWS, MLA_KV_LORA), F32)),
        compiler_params=_params("arbitrary"),
        name="decode_attention",
    )(page_table, qbd, lfnew, knew, vnew, qlat, qrope, ckvnew, krnew, kc, vc, ckvc, krc, lfc)
    return (o_fox.reshape(n_b * 4, FOX_Q_W).astype(BF),
            o_lat.reshape(n_b * 4, MLA_HEADS * MLA_KV_LORA).astype(BF))


def _mixout_kernel(x_ref, of_ref, ol_ref, wuv_ref, wo_ref, gpost_ref, gmpre_ref, wmq_ref,
                   x_out_ref, qm_ref):
    parts = [of_ref[...]]
    for hd in range(MLA_HEADS):
        om = _dot(ol_ref[:, hd * MLA_KV_LORA:(hd + 1) * MLA_KV_LORA], wuv_ref[hd])
        parts.append(om.astype(BF))
    o = jnp.concatenate(parts, axis=1)
    x = x_ref[...] + _rms(_dot(o, wo_ref[...]), gpost_ref[...])
    x_out_ref[...] = x
    hm = _rms(x, gmpre_ref[...]).astype(BF)
    qm_ref[...] = (_dot(hm, wmq_ref[...]) * MEM_SCALE).astype(BF)


def _mix_out(x, o_fox, o_lat, w):
    n = x.shape[0]
    tm = _row_tile(n)
    row = lambda width: pl.BlockSpec((tm, width), lambda i: (i, 0))
    mix_w = FOX_Q_W + MLA_HEADS * MLA_V_DIM
    return pl.pallas_call(
        _mixout_kernel,
        grid=(n // tm,),
        in_specs=[row(D_MODEL), row(FOX_Q_W), row(MLA_HEADS * MLA_KV_LORA),
                  _const_spec((MLA_HEADS, MLA_KV_LORA, MLA_V_DIM)),
                  _const_spec((mix_w, D_MODEL)), _const_spec((1, D_MODEL)),
                  _const_spec((1, D_MODEL)), _const_spec((D_MODEL, MEM_W))],
        out_specs=(row(D_MODEL), row(MEM_W)),
        out_shape=(jax.ShapeDtypeStruct((n, D_MODEL), F32),
                   jax.ShapeDtypeStruct((n, MEM_W), BF)),
        compiler_params=_params("parallel"),
        name="mix_out",
    )(x, o_fox, o_lat, w["w_uv"], w["w_out"], w["g_mix_post"], w["g_mem_pre"], w["w_mem_q"])


def _memkv_kernel(mem_ref, g_ref, w_ref, k_ref, v_ref):
    kv = _dot(_rms(mem_ref[...], g_ref[...]).astype(BF), w_ref[...])
    k_ref[...] = kv[:, :MEM_W]
    v_ref[...] = kv[:, MEM_W:]


def _memory_kv(mem, g, w):
    n = mem.shape[0]
    return pl.pallas_call(
        _memkv_kernel,
        grid=(1,),
        in_specs=[_const_spec((n, D_MODEL)), _const_spec((1, D_MODEL)),
                  _const_spec((D_MODEL, 2 * MEM_W))],
        out_specs=(_const_spec((n, MEM_W)), _const_spec((n, MEM_W))),
        out_shape=(jax.ShapeDtypeStruct((n, MEM_W), F32),) * 2,
        compiler_params=_params("arbitrary"),
        name="memory_kv",
    )(mem, g, w)


def _softmax_rows(s):
    p = jnp.exp(s - jnp.max(s, axis=1, keepdims=True))
    return p, jnp.sum(p, axis=1, keepdims=True)


def _mem_prompt_kernel(q_ref, k_ref, v_ref, o_ref):
    for hd in range(MEM_HEADS):
        sl = slice(hd * MEM_HEAD_DIM, (hd + 1) * MEM_HEAD_DIM)
        p, l = _softmax_rows(_dot_nt(q_ref[:, sl], k_ref[0, :, sl].astype(BF)))
        o = _dot(p.astype(BF), v_ref[0, :, sl].astype(BF))
        o_ref[:, sl] = (o / l).astype(BF)


def _mem_prompt(qm, mk, mv, seq_len):
    n = qm.shape[0]
    tm = _row_tile(seq_len)
    tps = seq_len // tm
    n_mem = mk.shape[1]
    kv = pl.BlockSpec((1, n_mem, MEM_W), lambda i: (i // tps, 0, 0))
    return pl.pallas_call(
        _mem_prompt_kernel,
        grid=(n // tm,),
        in_specs=[pl.BlockSpec((tm, MEM_W), lambda i: (i, 0)), kv, kv],
        out_specs=pl.BlockSpec((tm, MEM_W), lambda i: (i, 0)),
        out_shape=jax.ShapeDtypeStruct((n, MEM_W), BF),
        compiler_params=_params("parallel"),
        name="mem_prompt",
    )(qm, mk, mv)


def _mem_decode_kernel(q_ref, k_ref, v_ref, o_ref):
    rows, n_kv = q_ref.shape[1], k_ref.shape[1]
    r_head = lax.broadcasted_iota(jnp.int32, (rows, n_kv), 0) & (MEM_HEADS - 1)
    c_head = lax.broadcasted_iota(jnp.int32, (rows, n_kv), 1) & (MEM_HEADS - 1)
    own = r_head == c_head
    n_b = q_ref.shape[0]
    scores = [_dot_nt(q_ref[b], k_ref[b].astype(BF)) for b in range(n_b)]
    for b in range(n_b):
        p, l = _softmax_rows(jnp.where(own, scores[b], NEG_INF))
        o_ref[b] = (_dot(p.astype(BF), v_ref[b].astype(BF)) / l).astype(BF)


def _mem_decode(qm, cache_k, cache_v):
    n_b, n_mem = cache_k.shape[:2]
    assert MEM_HEADS & (MEM_HEADS - 1) == 0
    rows = 4 * MEM_HEADS
    n_kv = n_mem * MEM_HEADS
    g = MEM_DEC_BATCH if n_b % MEM_DEC_BATCH == 0 else n_b
    kv = pl.BlockSpec((g, n_kv, MEM_HEAD_DIM), lambda i: (i, 0, 0))
    qo = pl.BlockSpec((g, rows, MEM_HEAD_DIM), lambda i: (i, 0, 0))
    o = pl.pallas_call(
        _mem_decode_kernel,
        grid=(n_b // g,),
        in_specs=[qo, kv, kv],
        out_specs=qo,
        out_shape=jax.ShapeDtypeStruct((n_b, rows, MEM_HEAD_DIM), BF),
        compiler_params=_params("parallel"),
        name="mem_decode",
    )(qm.reshape(n_b, rows, MEM_HEAD_DIM), cache_k.reshape(n_b, n_kv, MEM_HEAD_DIM),
      cache_v.reshape(n_b, n_kv, MEM_HEAD_DIM))
    return o.reshape(n_b * 4, MEM_W)


def _prep_weights(l, g_ffn1_pre, w_ffn1_gu, w_ffn1_down, g_ffn1_post, g_mix_pre, w_in,
                  b_fgate, g_q_norm, w_q_up, g_kv_norm, w_kv_up, w_out, g_mix_post,
                  g_mem_tok, w_mem_kv, g_mem_pre, w_mem_q, w_mem_o, g_mem_post,
                  g_ffn2_pre, w_ffn2_gu, w_ffn2_down, g_ffn2_post):
    row = lambda g: g[l].reshape(1, -1)
    w = {}
    for name, (gpre, wgu, wd, gpost) in {
            "ffn1": (g_ffn1_pre, w_ffn1_gu, w_ffn1_down, g_ffn1_post),
            "ffn2": (g_ffn2_pre, w_ffn2_gu, w_ffn2_down, g_ffn2_post)}.items():
        w[name] = (row(gpre), wgu[l].astype(BF), wd[l].astype(BF), row(gpost))
    cuts = np.cumsum([FOX_Q_W, FOX_KV_W, FOX_KV_W, FOX_HEADS, MLA_Q_LORA, MLA_KV_LORA]).tolist()
    wq, wk, wv, wf, wcq, wckv, wkr = jnp.split(w_in[l], cuts, axis=1)
    half = MLA_ROPE_DIM // 2
    pad = lambda a: jnp.pad(a, ((0, 0), (0, LANES - a.shape[1])))
    wkr_rot = jnp.concatenate([wkr[:, half:], wkr[:, :half]], axis=1)
    w["g_mix_pre"] = row(g_mix_pre)
    w["w_big"] = jnp.concatenate([wq, wk, wv, wcq, wckv], axis=1).astype(BF)
    w["w_small"] = jnp.concatenate([pad(wkr), pad(wkr_rot), pad(wf)], axis=1).astype(BF)
    w["b_f"] = pad(b_fgate[l].reshape(1, -1))
    w["g_q_norm"] = row(g_q_norm)
    w["g_kv_norm"] = row(g_kv_norm)
    wqu = w_q_up[l].reshape(MLA_Q_LORA, MLA_HEADS, MLA_NOPE_DIM + MLA_ROPE_DIM)
    w["w_q_nope"] = wqu[..., :MLA_NOPE_DIM].reshape(MLA_Q_LORA, -1).astype(BF)
    wqr = wqu[..., MLA_NOPE_DIM:]
    wqr_rot = jnp.concatenate([wqr[..., half:], wqr[..., :half]], axis=-1)
    w["w_q_rope_a"] = wqr.reshape(MLA_Q_LORA, -1).astype(BF)
    w["w_q_rope_b"] = wqr_rot.reshape(MLA_Q_LORA, -1).astype(BF)
    wkv = w_kv_up[l].reshape(MLA_KV_LORA, MLA_HEADS, MLA_NOPE_DIM + MLA_V_DIM)
    w["w_uk_t"] = jnp.transpose(wkv[..., :MLA_NOPE_DIM], (1, 2, 0)).astype(BF)
    w["w_uv"] = jnp.transpose(wkv[..., MLA_NOPE_DIM:], (1, 0, 2)).astype(BF)
    w["w_out"] = w_out[l].astype(BF)
    w["g_mix_post"] = row(g_mix_post)
    w["g_mem_tok"] = row(g_mem_tok)
    w["w_mem_kv"] = w_mem_kv[l].astype(BF)
    w["g_mem_pre"] = row(g_mem_pre)
    w["w_mem_q"] = w_mem_q[l].astype(BF)
    w["w_mem_o"] = w_mem_o[l].astype(BF)
    w["g_mem_post"] = row(g_mem_post)
    return w


def _rope_tables(pos):
    half = MLA_ROPE_DIM // 2
    inv_freq = ROPE_THETA ** (-jnp.arange(half, dtype=F32) / half)
    ang = pos.astype(F32)[:, None] * inv_freq[None, :]
    cos, sin = jnp.cos(ang), jnp.sin(ang)
    cos_t = jnp.tile(jnp.concatenate([cos, cos], axis=1), (1, MLA_HEADS))
    sin_t = jnp.tile(jnp.concatenate([-sin, sin], axis=1), (1, MLA_HEADS))
    return cos_t, sin_t


def kernel(x_prompt, x_sample, mem_prompt, cache_fox_k, cache_fox_v, cache_fox_logf, cache_mla_ckv, cache_mla_krope, cache_mem_k, cache_mem_v, page_table, g_ffn1_pre, w_ffn1_gu, w_ffn1_down, g_ffn1_post, g_mix_pre, w_in, b_fgate, g_q_norm, w_q_up, g_kv_norm, w_kv_up, w_out, g_mix_post, g_mem_tok, w_mem_kv, g_mem_pre, w_mem_q, w_mem_o, g_mem_post, g_ffn2_pre, w_ffn2_gu, w_ffn2_down, g_ffn2_post):
    n_seq, seq_len, _ = x_prompt.shape
    n_dec, dec_seq, _ = x_sample.shape
    depth = w_in.shape[0]
    n_pages = page_table.shape[1]
    past_len = n_pages * PAGE_SIZE
    n_mem = mem_prompt.shape[1]
    assert dec_seq == 4

    xp = x_prompt.reshape(n_seq * seq_len, D_MODEL)
    xs = x_sample.reshape(n_dec * dec_seq, D_MODEL)
    mem = mem_prompt.reshape(n_seq * n_mem, D_MODEL)
    cos_p, sin_p = _rope_tables(jnp.arange(seq_len, dtype=jnp.int32))
    pos_s = past_len + jnp.arange(dec_seq, dtype=jnp.int32)
    cos_s, sin_s = _rope_tables(jnp.tile(pos_s, n_dec))

    outs = {k: [] for k in ("pk", "pv", "plf", "pckv", "pkr", "pmk", "pmv",
                            "sk", "sv", "slf", "sckv", "skr")}
    for l in range(depth):
        w = _prep_weights(l, g_ffn1_pre, w_ffn1_gu, w_ffn1_down, g_ffn1_post, g_mix_pre,
                          w_in, b_fgate, g_q_norm, w_q_up, g_kv_norm, w_kv_up, w_out,
                          g_mix_post, g_mem_tok, w_mem_kv, g_mem_pre, w_mem_q, w_mem_o,
                          g_mem_post, g_ffn2_pre, w_ffn2_gu, w_ffn2_down, g_ffn2_post)

        xp = _ffn_half(xp, *w["ffn1"])
        pp = _mix_proj(xp, seq_len, w, cos_p, sin_p, prompt=True)
        o_fox = _fox_prompt(pp, n_seq, seq_len)
        o_lat = _mla_prompt(pp, n_seq, seq_len)
        xp, qm = _mix_out(xp, o_fox, o_lat, w)
        mk, mv = _memory_kv(mem, w["g_mem_tok"], w["w_mem_kv"])
        om = _mem_prompt(qm, mk.reshape(n_seq, n_mem, MEM_W), mv.reshape(n_seq, n_mem, MEM_W),
                         seq_len)
        xp = _memout_ffn(xp, om, w["w_mem_o"], w["g_mem_post"], *w["ffn2"])
        heads_last = lambda a: jnp.transpose(
            a.reshape(n_seq, FOX_KV_HEADS, FOX_HEAD_DIM, seq_len), (0, 3, 1, 2))
        outs["pk"].append(heads_last(pp["kT32"]))
        outs["pv"].append(heads_last(pp["vT32"]))
        outs["plf"].append(jnp.transpose(pp["lfT"], (0, 2, 1)))
        outs["pckv"].append(pp["ckv32"].reshape(n_seq, seq_len, MLA_KV_LORA))
        outs["pkr"].append(jnp.transpose(pp["krT32"], (0, 2, 1)))
        outs["pmk"].append(mk.reshape(n_seq, n_mem, MEM_HEADS, MEM_HEAD_DIM))
        outs["pmv"].append(mv.reshape(n_seq, n_mem, MEM_HEADS, MEM_HEAD_DIM))

        xs = _ffn_half(xs, *w["ffn1"])
        ps = _mix_proj(xs, n_dec * dec_seq, w, cos_s, sin_s, prompt=False)
        n_pool = cache_fox_k.shape[1]
        kc = jnp.transpose(cache_fox_k[l], (0, 2, 3, 1)).reshape(n_pool, FOX_KV_W, PAGE_SIZE)
        vc = jnp.transpose(cache_fox_v[l], (0, 2, 3, 1)).reshape(n_pool, FOX_KV_W, PAGE_SIZE)
        krc = jnp.transpose(cache_mla_krope[l], (0, 2, 1))
        lfc = jnp.transpose(cache_fox_logf[l], (0, 2, 1))
        o_fox, o_lat = _decode_attention(ps, page_table, kc, vc, cache_mla_ckv[l], krc, lfc)
        xs, qm = _mix_out(xs, o_fox, o_lat, w)
        om = _mem_decode(qm, cache_mem_k[l], cache_mem_v[l])
        xs = _memout_ffn(xs, om, w["w_mem_o"], w["g_mem_post"], *w["ffn2"])
        outs["sk"].append(ps["k32"].reshape(n_dec, dec_seq, FOX_KV_HEADS, FOX_HEAD_DIM))
        outs["sv"].append(ps["v32"].reshape(n_dec, dec_seq, FOX_KV_HEADS, FOX_HEAD_DIM))
        outs["slf"].append(ps["lf"].reshape(n_dec, dec_seq, FOX_HEADS))
        outs["sckv"].append(ps["ckv32"].reshape(n_dec, dec_seq, MLA_KV_LORA))
        outs["skr"].append(ps["kr32"].reshape(n_dec, dec_seq, MLA_ROPE_DIM))

    st = {k: jnp.stack(v) for k, v in outs.items()}
    return (xp.reshape(n_seq, seq_len, D_MODEL), xs.reshape(n_dec, dec_seq, D_MODEL),
            st["pk"], st["pv"], st["plf"], st["pckv"], st["pkr"], st["pmk"], st["pmv"],
            st["sk"], st["sv"], st["slf"], st["sckv"], st["skr"])
```
